```python
import math
import jax, jax.numpy as jnp
from jax import lax
import numpy as np

D_MODEL = 2048
BATCH = 2
SEQ = 16384
DEPTH = 4

N_MIXERS = 3
D_FF = 5632
N_MOD = 9
NORM_EPS = 1e-6

SSM_GROUP = 16
SSM_GROUPS = D_MODEL // SSM_GROUP
SSM_STATE = 64
SSM_CHUNK = 128
SSM_DT_MIN = 1e-3
SSM_DT_MAX = 1e-1

GLA_HEADS = 4
GLA_DK = D_MODEL // 2
GLA_DV = D_MODEL
GLA_DKH = GLA_DK // GLA_HEADS
GLA_DVH = GLA_DV // GLA_HEADS
GLA_RANK = 16
GLA_TAU = 16.0
GLA_CHUNK = 64

DSA_PATTERNS = ((128, 1), (512, 4), (2048, 16))
DSA_GROUPS = 3
DSA_HEADS = 8
DSA_HEAD_DIM = 128
DSA_QKV = 3 * DSA_GROUPS * DSA_HEADS * DSA_HEAD_DIM
DSA_OUT = DSA_HEADS * DSA_HEAD_DIM
DSA_QBLOCK = 128
ROPE_THETA = 10000.0

kernel_name = "hybrid_s5_gla_dilated_macaron_encoder"


def _rms(x):
    xf = x.astype(jnp.float32)
    return xf * lax.rsqrt(jnp.mean(xf * xf, axis=-1, keepdims=True) + NORM_EPS)


def _modulate(x, g, shift, scale):
    h = _rms(x).astype(x.dtype) * g
    return h * (1 + scale) + shift


def _swiglu(h, wi, wo):
    g, u = jnp.split(h @ wi, 2, axis=-1)
    return (jax.nn.silu(g) * u) @ wo


def _cmul(ar, ai, br, bi):
    return ar * br - ai * bi, ar * bi + ai * br


def _s5_direction(ug, lam_re, lam_im, log_dt, b_re, b_im, c_re, c_im):
    bsz, L = ug.shape[0], ug.shape[1]
    G, K, P = SSM_GROUPS, SSM_GROUP, SSM_STATE
    lr = lam_re.astype(jnp.float32)
    li = lam_im.astype(jnp.float32)
    dt = jnp.exp(log_dt.astype(jnp.float32))[:, None]
    mag = jnp.exp(lr * dt)
    ar, ai = mag * jnp.cos(li * dt), mag * jnp.sin(li * dt)
    den = lr * lr + li * li
    fr = ((ar - 1.0) * lr + ai * li) / den
    fi = (ai * lr - (ar - 1.0) * li) / den
    bbr, bbi = _cmul(fr[..., None], fi[..., None],
                     b_re.astype(jnp.float32), b_im.astype(jnp.float32))
    cr_w, ci_w = c_re.astype(jnp.float32), c_im.astype(jnp.float32)
    n_chunk = L // SSM_CHUNK
    uc = ug.reshape(bsz, n_chunk, SSM_CHUNK, G, K).transpose(1, 2, 0, 3, 4)
    a_r = jnp.broadcast_to(ar, (SSM_CHUNK, bsz, G, P))
    a_i = jnp.broadcast_to(ai, (SSM_CHUNK, bsz, G, P))

    def combine(e1, e2):
        a1r, a1i, b1r, b1i = e1
        a2r, a2i, b2r, b2i = e2
        nar, nai = _cmul(a2r, a2i, a1r, a1i)
        nbr, nbi = _cmul(a2r, a2i, b1r, b1i)
        return (nar, nai, nbr + b2r, nbi + b2i)

    def step(carry, u_t):
        hr, hi = carry
        bur = jnp.einsum('tbgk,gpk->tbgp', u_t, bbr)
        bui = jnp.einsum('tbgk,gpk->tbgp', u_t, bbi)
        pr, pi, sr, si = lax.associative_scan(combine, (a_r, a_i, bur, bui), axis=0)
        cr, ci = _cmul(pr, pi, hr[None], hi[None])
        sr = sr + cr
        si = si + ci
        y = jnp.einsum('tbgp,gkp->tbgk', sr, cr_w) - jnp.einsum('tbgp,gkp->tbgk', si, ci_w)
        return (sr[-1], si[-1]), y

    init = (jnp.zeros((bsz, G, P), jnp.float32), jnp.zeros((bsz, G, P), jnp.float32))
    _, y = lax.scan(step, init, uc)
    return y.transpose(2, 0, 1, 3, 4).reshape(bsz, L, G * K)


def _s5_mixer(h, w_in, lam_re, lam_im, log_dt, b_re, b_im, c_re, c_im, d_skip, w_out):
    bsz, L, _ = h.shape
    u = h @ w_in
    ug = u.astype(jnp.float32).reshape(bsz, L, SSM_GROUPS, SSM_GROUP)
    y_f = _s5_direction(ug, lam_re[0], lam_im[0], log_dt[0], b_re[0], b_im[0], c_re[0], c_im[0])
    y_b = jnp.flip(_s5_direction(jnp.flip(ug, 1), lam_re[1], lam_im[1], log_dt[1],
                                 b_re[1], b_im[1], c_re[1], c_im[1]), 1)
    y = y_f + y_b + d_skip.astype(jnp.float32) * u.astype(jnp.float32)
    y = jax.nn.gelu(y).astype(h.dtype)
    a, g = jnp.split(y @ w_out, 2, axis=-1)
    return a * jax.nn.sigmoid(g)


def _gla_direction(q, k, v, logg, strict):
    bsz, L, H, dk = q.shape
    dv = v.shape[-1]
    n = L // GLA_CHUNK

    def chunks(t):
        return t.reshape(bsz, n, GLA_CHUNK, H, t.shape[-1]).transpose(1, 0, 3, 2, 4)

    qc, kc, vc, gc = chunks(q), chunks(k), chunks(v), chunks(logg)
    bc = jnp.cumsum(gc, axis=3)
    t_idx = jnp.arange(GLA_CHUNK)
    mask = (t_idx[:, None] > t_idx[None, :]) if strict else (t_idx[:, None] >= t_idx[None, :])

    def step(S, inp):
        q_, k_, v_, b_ = inp
        qd = q_ * jnp.exp(b_)
        kd = k_ * jnp.exp(-b_)
        att = jnp.where(mask, jnp.einsum('bhtk,bhsk->bhts', qd, kd), 0.0)
        o = jnp.einsum('bhts,bhsv->bhtv', att, v_) + jnp.einsum('bhtk,bhkv->bhtv', qd, S)
        b_last = b_[:, :, -1:, :]
        kl = k_ * jnp.exp(b_last - b_)
        S = jnp.exp(b_last[:, :, 0, :])[..., None] * S + jnp.einsum('bhsk,bhsv->bhkv', kl, v_)
        return S, o

    S0 = jnp.zeros((bsz, H, dk, dv), jnp.float32)
    _, o = lax.scan(step, S0, (qc, kc, vc, bc))
    return o.transpose(1, 0, 3, 2, 4).reshape(bsz, L, H, dv)


def _gla_mixer(h, w_in, w_gate, b_gate, norm_g, w_o):
    bsz, L, _ = h.shape
    proj = h @ w_in
    q, k, v, r, glr = jnp.split(proj, [GLA_DK, 2 * GLA_DK, 2 * GLA_DK + GLA_DV,
                                       2 * GLA_DK + 2 * GLA_DV], axis=-1)
    q = q.astype(jnp.float32).reshape(bsz, L, GLA_HEADS, GLA_DKH) * (GLA_DKH ** -0.5)
    k = k.astype(jnp.float32).reshape(bsz, L, GLA_HEADS, GLA_DKH)
    v = v.astype(jnp.float32).reshape(bsz, L, GLA_HEADS, GLA_DVH)

    def log_gate(d):
        gl = glr[..., d * GLA_RANK:(d + 1) * GLA_RANK] @ w_gate[d] + b_gate[d]
        return (jax.nn.log_sigmoid(gl.astype(jnp.float32)) / GLA_TAU).reshape(bsz, L, GLA_HEADS, GLA_DKH)

    o_f = _gla_direction(q, k, v, log_gate(0), False)
    o_b = jnp.flip(_gla_direction(jnp.flip(q, 1), jnp.flip(k, 1), jnp.flip(v, 1),
                                  jnp.flip(log_gate(1), 1), True), 1)
    o = _rms(o_f + o_b) * norm_g.astype(jnp.float32).reshape(GLA_HEADS, GLA_DVH)
    o = o.reshape(bsz, L, GLA_DV).astype(h.dtype) * jax.nn.silu(r)
    return o @ w_o


def _rope_tables(L):
    pos = jnp.arange(L, dtype=jnp.float32)
    inv = ROPE_THETA ** (-jnp.arange(0, DSA_HEAD_DIM, 2, dtype=jnp.float32) / DSA_HEAD_DIM)
    ang = pos[:, None] * inv[None, :]
    return jnp.cos(ang), jnp.sin(ang)


def _apply_rope(x, cos, sin):
    x1, x2 = jnp.split(x, 2, axis=-1)
    c = cos[None, :, None, None, :]
    s = sin[None, :, None, None, :]
    return jnp.concatenate([x1 * c - x2 * s, x2 * c + x1 * s], axis=-1)


def _dilated_mixer(h, w_in, q_norm, k_norm, w_o):
    bsz, L, _ = h.shape
    proj = (h @ w_in).reshape(bsz, L, 3, DSA_GROUPS, DSA_HEADS, DSA_HEAD_DIM)
    cos, sin = _rope_tables(L)
    qn = q_norm.astype(jnp.float32)[:, None, :]
    kn = k_norm.astype(jnp.float32)[:, None, :]
    q = _apply_rope(_rms(proj[:, :, 0]) * qn, cos, sin) * (DSA_HEAD_DIM ** -0.5)
    k = _apply_rope(_rms(proj[:, :, 1]) * kn, cos, sin)
    q = q.astype(h.dtype).transpose(2, 0, 3, 1, 4)
    k = k.astype(h.dtype).transpose(2, 0, 3, 1, 4)
    v = proj[:, :, 2].transpose(2, 0, 3, 1, 4)
    n_blk = L // DSA_QBLOCK

    def block(j):
        t = j * DSA_QBLOCK + jnp.arange(DSA_QBLOCK)
        outs, lses = [], []
        for g, (window, dil) in enumerate(DSA_PATTERNS):
            n_side = window // (2 * dil)
            offs = dil * jnp.arange(-n_side, n_side + 1)
            idx = t[:, None] + offs[None, :]
            valid = (idx >= 0) & (idx < L)
            idxc = jnp.clip(idx, 0, L - 1)
            qb = lax.dynamic_slice_in_dim(q[g], j * DSA_QBLOCK, DSA_QBLOCK, axis=2)
            kg = jnp.take(k[g], idxc, axis=2)
            vg = jnp.take(v[g], idxc, axis=2)
            s = jnp.einsum('bhqd,bhqnd->bhqn', qb, kg).astype(jnp.float32)
            s = jnp.where(valid, s, -1e30)
            m = jnp.max(s, axis=-1, keepdims=True)
            p = jnp.exp(s - m)
            den = jnp.sum(p, axis=-1, keepdims=True)
            o = jnp.einsum('bhqn,bhqnd->bhqd', p.astype(vg.dtype), vg).astype(jnp.float32) / den
            outs.append(o)
            lses.append((m + jnp.log(den))[..., 0])
        w = jax.nn.softmax(jnp.stack(lses, 0), axis=0)
        return jnp.sum(w[..., None] * jnp.stack(outs, 0), axis=0).astype(h.dtype)

    o = lax.map(block, jnp.arange(n_blk))
    o = o.transpose(1, 0, 3, 2, 4).reshape(bsz, L, DSA_OUT)
    return o @ w_o


def setup_inputs(seed: int = 0) -> dict:
    key = jax.random.key(seed)
    ks = iter(jax.random.split(key, 40))
    f32 = jnp.float32
    D, G, P, K = D_MODEL, SSM_GROUPS, SSM_STATE, SSM_GROUP
    n_a = len(range(0, DEPTH, N_MIXERS))
    n_b = len(range(1, DEPTH, N_MIXERS))
    n_c = len(range(2, DEPTH, N_MIXERS))

    def nrm(shape, scale):
        return jax.random.normal(next(ks), shape, f32) * scale

    inp = {}
    inp['x'] = nrm((BATCH, SEQ, D), 1.0)
    inp['c'] = nrm((BATCH, D), 1.0)
    inp['ada_w'] = nrm((DEPTH, D, N_MOD * D), 0.5 * D ** -0.5)
    inp['ada_b'] = nrm((DEPTH, N_MOD * D), 0.02)
    inp['norm_g'] = 1.0 + nrm((DEPTH, 3, D), 0.02)
    inp['ffn1_wi'] = nrm((DEPTH, D, 2 * D_FF), D ** -0.5)
    inp['ffn1_wo'] = nrm((DEPTH, D_FF, D), D_FF ** -0.5)
    inp['ffn2_wi'] = nrm((DEPTH, D, 2 * D_FF), D ** -0.5)
    inp['ffn2_wo'] = nrm((DEPTH, D_FF, D), D_FF ** -0.5)
    inp['ssm_w_in'] = nrm((n_a, D, D), D ** -0.5)
    inp['ssm_lam_re'] = -0.5 + nrm((n_a, 2, G, P), 0.01)
    inp['ssm_lam_im'] = math.pi * jnp.arange(P, dtype=f32) + nrm((n_a, 2, G, P), 0.01)
    inp['ssm_log_dt'] = jax.random.uniform(next(ks), (n_a, 2, G), f32,
                                           math.log(SSM_DT_MIN), math.log(SSM_DT_MAX))
    inp['ssm_b_re'] = nrm((n_a, 2, G, P, K), (2 * K) ** -0.5)
    inp['ssm_b_im'] = nrm((n_a, 2, G, P, K), (2 * K) ** -0.5)
    inp['ssm_c_re'] = nrm((n_a, 2, G, K, P), (2 * P) ** -0.5)
    inp['ssm_c_im'] = nrm((n_a, 2, G, K, P), (2 * P) ** -0.5)
    inp['ssm_d'] = nrm((n_a, D), 1.0)
    inp['ssm_w_out'] = nrm((n_a, D, 2 * D), D ** -0.5)
    inp['gla_w_in'] = nrm((n_b, D, 2 * GLA_DK + 2 * GLA_DV + 2 * GLA_RANK), D ** -0.5)
    inp['gla_w_gate'] = nrm((n_b, 2, GLA_RANK, GLA_DK), GLA_RANK ** -0.5)
    inp['gla_b_gate'] = nrm((n_b, 2, GLA_DK), 0.01)
    inp['gla_norm_g'] = 1.0 + nrm((n_b, GLA_DV), 0.02)
    inp['gla_w_o'] = nrm((n_b, GLA_DV, D), GLA_DV ** -0.5)
    inp['dsa_w_in'] = nrm((n_c, D, DSA_QKV), D ** -0.5)
    inp['dsa_q_norm'] = 1.0 + nrm((n_c, DSA_GROUPS, DSA_HEAD_DIM), 0.02)
    inp['dsa_k_norm'] = 1.0 + nrm((n_c, DSA_GROUPS, DSA_HEAD_DIM), 0.02)
    inp['dsa_w_o'] = nrm((n_c, DSA_OUT, D), DSA_OUT ** -0.5)
    return inp


def reference(x, c, ada_w, ada_b, norm_g, ffn1_wi, ffn1_wo, ffn2_wi, ffn2_wo,
              ssm_w_in, ssm_lam_re, ssm_lam_im, ssm_log_dt, ssm_b_re, ssm_b_im,
              ssm_c_re, ssm_c_im, ssm_d, ssm_w_out,
              gla_w_in, gla_w_gate, gla_b_gate, gla_norm_g, gla_w_o,
              dsa_w_in, dsa_q_norm, dsa_k_norm, dsa_w_o):
    bsz = x.shape[0]
    c_act = jax.nn.silu(c)
    for i in range(DEPTH):
        mod = (c_act @ ada_w[i] + ada_b[i]).reshape(bsz, N_MOD, 1, D_MODEL)
        sh1, sc1, g1 = mod[:, 0], mod[:, 1], mod[:, 2]
        shm, scm, gm = mod[:, 3], mod[:, 4], mod[:, 5]
        sh2, sc2, g2 = mod[:, 6], mod[:, 7], mod[:, 8]
        h = _modulate(x, norm_g[i, 0], sh1, sc1)
        x = x + 0.5 * g1 * _swiglu(h, ffn1_wi[i], ffn1_wo[i])
        h = _modulate(x, norm_g[i, 1], shm, scm)
        kind, j = i % N_MIXERS, i // N_MIXERS
        if kind == 0:
            y = _s5_mixer(h, ssm_w_in[j], ssm_lam_re[j], ssm_lam_im[j], ssm_log_dt[j],
                          ssm_b_re[j], ssm_b_im[j], ssm_c_re[j], ssm_c_im[j], ssm_d[j], ssm_w_out[j])
        elif kind == 1:
            y = _gla_mixer(h, gla_w_in[j], gla_w_gate[j], gla_b_gate[j], gla_norm_g[j], gla_w_o[j])
        else:
            y = _dilated_mixer(h, dsa_w_in[j], dsa_q_norm[j], dsa_k_norm[j], dsa_w_o[j])
        x = x + gm * y
        h = _modulate(x, norm_g[i, 2], sh2, sc2)
        x = x + 0.5 * g2 * _swiglu(h, ffn2_wi[i], ffn2_wo[i])
    return x
```

```python
import functools
import math

import jax
import jax.numpy as jnp
from jax import lax
from jax.experimental import pallas as pl
from jax.experimental.pallas import tpu as pltpu

F32 = jnp.float32
BF16 = jnp.bfloat16

D_MODEL = 2048
D_FF = 5632
N_MOD = 9
NORM_EPS = 1e-6

SSM_GROUP = 16
SSM_GROUPS = D_MODEL // SSM_GROUP
SSM_STATE = 64
SSM_SUB = 16
SSM_FLAT = SSM_SUB * SSM_GROUP

GLA_HEADS = 4
GLA_DK = D_MODEL // 2
GLA_DV = D_MODEL
GLA_DKH = GLA_DK // GLA_HEADS
GLA_DVH = GLA_DV // GLA_HEADS
GLA_RANK = 16
GLA_TAU = 16.0
GLA_CHUNK = 64

DSA_PATTERNS = ((128, 1), (512, 4), (2048, 16))
DSA_GROUPS = 3
DSA_HEADS = 8
DSA_HEAD_DIM = 128
DSA_GW = DSA_HEADS * DSA_HEAD_DIM
DSA_QKV = 3 * DSA_GROUPS * DSA_GW
DSA_SIDE = 64
DSA_QSUB = 128
ROPE_THETA = 10000.0

V7X_VMEM_LIMIT = 56 * 1024 * 1024


def _cparams(*sem):
    return pltpu.CompilerParams(dimension_semantics=sem, vmem_limit_bytes=V7X_VMEM_LIMIT)


def _pick(n, pref):
    t = min(n, pref)
    while n % t:
        t //= 2
    return t


def _silu(v):
    return v * jax.nn.sigmoid(v)


def _dot(a, b):
    return jnp.dot(a, b, preferred_element_type=F32)


def _dot_nt(a, b):
    return lax.dot_general(a, b, (((1,), (1,)), ((), ())), preferred_element_type=F32)


def _dot_tn(a, b):
    return lax.dot_general(a, b, (((0,), (0,)), ((), ())), preferred_element_type=F32)


def _split_bf16(v):
    hi = v.astype(BF16)
    lo = (v - hi.astype(F32)).astype(BF16)
    return hi, lo


def _ada_kernel(c_ref, w_ref, b_ref, o_ref):
    hi, lo = _split_bf16(_silu(c_ref[...]))
    w = w_ref[0].astype(BF16)
    o_ref[0] = _dot(hi, w) + _dot(lo, w) + b_ref[0]


def _ada(c, ada_w, ada_b):
    depth, d, n = ada_w.shape
    bsz = c.shape[0]
    rows = 8
    c_pad = jnp.zeros((rows, d), F32).at[:bsz].set(c)
    tn = _pick(n, 1024)
    out = pl.pallas_call(
        _ada_kernel,
        grid=(depth, n // tn),
        in_specs=[
            pl.BlockSpec((rows, d), lambda i, j: (0, 0)),
            pl.BlockSpec((1, d, tn), lambda i, j: (i, 0, j)),
            pl.BlockSpec((1, 1, tn), lambda i, j: (i, 0, j)),
        ],
        out_specs=pl.BlockSpec((1, rows, tn), lambda i, j: (i, 0, j)),
        out_shape=jax.ShapeDtypeStruct((depth, rows, n), F32),
        compiler_params=_cparams("parallel", "parallel"),
        name="ada_mod",
    )(c_pad, ada_w, ada_b.reshape(depth, 1, n))
    return out[:, :bsz].reshape(depth, bsz, N_MOD, d)


def _norm_mod(x_ref, mod_ref, g_ref, h_ref, shift_idx):
    tm = x_ref.shape[0]
    rows = _pick(tm, 256)
    shift = mod_ref[0, shift_idx:shift_idx + 1, :]
    scale1 = 1.0 + mod_ref[0, shift_idx + 1:shift_idx + 2, :]
    g = g_ref[...]

    def body(r, carry):
        sl = pl.ds(pl.multiple_of(r * rows, rows), rows)
        x = x_ref[sl, :]
        ms = jnp.mean(x * x, axis=-1, keepdims=True)
        h = (x * lax.rsqrt(ms + NORM_EPS)) * g
        h_ref[sl, :] = (h * scale1 + shift).astype(BF16)
        return carry

    lax.fori_loop(0, tm // rows, body, 0)


def _nm_specs(tm, d, seq):
    blocks_per_seq = seq // tm
    return [
        pl.BlockSpec((tm, d), lambda i, j: (i, 0)),
        pl.BlockSpec((1, N_MOD, d), lambda i, j: (i // blocks_per_seq, 0, 0)),
        pl.BlockSpec((1, d), lambda i, j: (0, 0)),
    ]


def _ffn_in_kernel(x_ref, mod_ref, g_ref, wg_ref, wu_ref, o_ref, h_ref, *, shift_idx):
    @pl.when(pl.program_id(1) == 0)
    def _():
        _norm_mod(x_ref, mod_ref, g_ref, h_ref, shift_idx)

    h = h_ref[...]
    a = _dot(h, wg_ref[...])
    u = _dot(h, wu_ref[...])
    o_ref[...] = (_silu(a) * u).astype(o_ref.dtype)


def _ffn_in(x2d, mod, gain, wi, seq, shift_idx):
    m, d = x2d.shape
    f = wi.shape[1] // 2
    tm = _pick(seq, 1024)
    tf = _pick(f, 512)
    nf = f // tf
    return pl.pallas_call(
        functools.partial(_ffn_in_kernel, shift_idx=shift_idx),
        grid=(m // tm, nf),
        in_specs=_nm_specs(tm, d, seq) + [
            pl.BlockSpec((d, tf), lambda i, j: (0, j)),
            pl.BlockSpec((d, tf), lambda i, j: (0, j + nf)),
        ],
        out_specs=pl.BlockSpec((tm, tf), lambda i, j: (i, j)),
        out_shape=jax.ShapeDtypeStruct((m, f), BF16),
        scratch_shapes=[pltpu.VMEM((tm, d), BF16)],
        compiler_params=_cparams("parallel", "arbitrary"),
        name="ffn_in",
    )(x2d, mod, gain, wi, wi)


def _ffn_out_kernel(a_ref, w_ref, x_ref, mod_ref, o_ref, *, gate_idx):
    y = _dot(a_ref[...], w_ref[...])
    gate = mod_ref[0, gate_idx:gate_idx + 1, :]
    o_ref[...] = x_ref[...] + (0.5 * gate) * y


def _ffn_out(act, wo, x2d, mod, seq, gate_idx):
    m, f = act.shape
    d = wo.shape[1]
    tm = _pick(seq, 1024)
    tn = _pick(d, 512)
    blocks_per_seq = seq // tm
    return pl.pallas_call(
        functools.partial(_ffn_out_kernel, gate_idx=gate_idx),
        grid=(m // tm, d // tn),
        in_specs=[
            pl.BlockSpec((tm, f), lambda i, j: (i, 0)),
            pl.BlockSpec((f, tn), lambda i, j: (0, j)),
            pl.BlockSpec((tm, tn), lambda i, j: (i, j)),
            pl.BlockSpec((1, N_MOD, tn), lambda i, j: (i // blocks_per_seq, 0, j)),
        ],
        out_specs=pl.BlockSpec((tm, tn), lambda i, j: (i, j)),
        out_shape=jax.ShapeDtypeStruct((m, d), F32),
        compiler_params=_cparams("parallel", "parallel"),
        name="ffn_out",
    )(act, wo, x2d, mod)


def _proj_kernel(x_ref, mod_ref, g_ref, w_ref, o_ref, h_ref, *, shift_idx):
    @pl.when(pl.program_id(1) == 0)
    def _():
        _norm_mod(x_ref, mod_ref, g_ref, h_ref, shift_idx)

    o_ref[...] = _dot(h_ref[...], w_ref[...]).astype(o_ref.dtype)


def _proj(x2d, mod, gain, w, seq, shift_idx, out_dtype):
    m, d = x2d.shape
    n = w.shape[1]
    tm = _pick(seq, 1024)
    tn = _pick(n, 1024)
    return pl.pallas_call(
        functools.partial(_proj_kernel, shift_idx=shift_idx),
        grid=(m // tm, n // tn),
        in_specs=_nm_specs(tm, d, seq) + [pl.BlockSpec((d, tn), lambda i, j: (0, j))],
        out_specs=pl.BlockSpec((tm, tn), lambda i, j: (i, j)),
        out_shape=jax.ShapeDtypeStruct((m, n), out_dtype),
        scratch_shapes=[pltpu.VMEM((tm, d), BF16)],
        compiler_params=_cparams("parallel", "arbitrary"),
        name="proj",
    )(x2d, mod, gain, w)


def _s5_weights(lam_re, lam_im, log_dt, b_re, b_im, c_re, c_im, d_skip, seg_len):
    t_sub, g_n, p_n, k_n = SSM_SUB, SSM_GROUPS, SSM_STATE, SSM_GROUP
    hp = lax.Precision.HIGHEST
    lr, li = lam_re.astype(F32), lam_im.astype(F32)
    dt = jnp.exp(log_dt.astype(F32))[..., None]
    mag = jnp.exp(lr * dt)
    ar, ai = mag * jnp.cos(li * dt), mag * jnp.sin(li * dt)
    den = lr * lr + li * li
    fr = ((ar - 1.0) * lr + ai * li) / den
    fi = (ai * lr - (ar - 1.0) * li) / den
    bbr = fr[..., None] * b_re - fi[..., None] * b_im
    bbi = fr[..., None] * b_im + fi[..., None] * b_re
    n = jnp.arange(t_sub + 1, dtype=F32)[:, None, None, None]
    pmag = jnp.exp(n * (lr * dt))
    pr, pi = pmag * jnp.cos(n * (li * dt)), pmag * jnp.sin(n * (li * dt))
    cr, ci = c_re.astype(F32), c_im.astype(F32)

    prt, pit = pr[:t_sub, :, :, None, :], pi[:t_sub, :, :, None, :]
    clr = cr[None] * prt - ci[None] * pit
    cli = cr[None] * pit + ci[None] * prt
    taps = (jnp.einsum('tdgkp,dgpj->tdgkj', clr, bbr, precision=hp)
            - jnp.einsum('tdgkp,dgpj->tdgkj', cli, bbi, precision=hp))
    s_idx = jnp.arange(t_sub)[:, None]
    t_idx = jnp.arange(t_sub)[None, :]
    tau_f = t_idx - s_idx
    kf = jnp.where((tau_f >= 0)[:, :, None, None, None], taps[jnp.clip(tau_f, 0, t_sub - 1), 0], 0.0)
    kb = jnp.where((tau_f <= 0)[:, :, None, None, None], taps[jnp.clip(-tau_f, 0, t_sub - 1), 1], 0.0)
    m_mat = (kf + kb).transpose(2, 0, 4, 1, 3).reshape(g_n, SSM_FLAT, SSM_FLAT)
    skip = jnp.tile(d_skip.astype(F32).reshape(g_n, 1, k_n), (1, t_sub, 1)).reshape(g_n, SSM_FLAT)
    m_mat = m_mat + jnp.eye(SSM_FLAT, dtype=F32)[None] * skip[:, None, :]

    def inc(pw_r, pw_i, d):
        wr = pw_r[..., None] * bbr[d][None] - pw_i[..., None] * bbi[d][None]
        wi = pw_r[..., None] * bbi[d][None] + pw_i[..., None] * bbr[d][None]
        to = lambda a: a.transpose(1, 0, 3, 2).reshape(g_n, SSM_FLAT, p_n)
        return to(wr), to(wi)

    wfr, wfi = inc(pr[:t_sub, 0][::-1], pi[:t_sub, 0][::-1], 0)
    wbr, wbi = inc(pr[:t_sub, 1], pi[:t_sub, 1], 1)
    w_mat = jnp.concatenate([wfr, wfi, wfi, wfr, wbr, wbi, wbi, wbr], axis=-1)

    def outw(pw_r, pw_i, d):
        er = cr[d][None] * pw_r[:, :, None, :] - ci[d][None] * pw_i[:, :, None, :]
        ei = cr[d][None] * pw_i[:, :, None, :] + ci[d][None] * pw_r[:, :, None, :]
        to = lambda a: a.transpose(1, 3, 0, 2).reshape(g_n, p_n, SSM_FLAT)
        return to(er), to(-ei)

    vfr, vfi = outw(pr[1:, 0], pi[1:, 0], 0)
    vbr, vbi = outw(pr[1:, 1][::-1], pi[1:, 1][::-1], 1)
    v_mat = jnp.concatenate([vfr, vfi, vbr, vbi], axis=1)

    cat = lambda a, b: jnp.concatenate([a, b], axis=-1)

    def step_rows(a_r, a_i):
        return [cat(a_r, a_r), cat(-a_i, a_i), cat(a_i, -a_i)]

    span = float(t_sub * seg_len)
    smag = jnp.exp(span * (lr * dt))
    sr, si = smag * jnp.cos(span * (li * dt)), smag * jnp.sin(span * (li * dt))
    zero = jnp.zeros((g_n, 2 * p_n), F32)
    coef = jnp.stack(step_rows(pr[t_sub, 0], pi[t_sub, 0]) + step_rows(pr[t_sub, 1], pi[t_sub, 1])
                     + step_rows(sr[0], si[0]) + step_rows(sr[1], si[1]) + [zero] * 4, axis=1)
    mw = jnp.concatenate([m_mat, w_mat], axis=-1).astype(BF16)
    return mw, v_mat.astype(BF16), coef


S5_SEGMENTS = 8
S5_GROUPS_PER_STEP = 2


def _s5_step(v1, v2, s1, s2, c1, c2, c3):
    return c1 * v1 + c2 * v2 + s1, c1 * v2 + c3 * v1 + s2


def _s5_kernel(u_ref, mw_ref, v_ref, coef_ref, y_ref, s_ref, h_ref, *, seg_len, segs_per_batch):
    n_g = u_ref.shape[0]
    for g in range(n_g):
        u = u_ref[g]
        y_ref[g] = _dot(u, mw_ref[g, :, 0:SSM_FLAT])
        s_ref[g] = _dot(u, mw_ref[g, :, SSM_FLAT:])

    def rows3(g, r):
        return [jnp.broadcast_to(coef_ref[g, r + i:r + i + 1, :], (S5_SEGMENTS, 128)) for i in range(3)]

    cf = [rows3(g, 0) for g in range(n_g)]
    cb = [rows3(g, 3) for g in range(n_g)]

    def tile(ref, g, c, lane0):
        return ref[g, pl.ds(pl.multiple_of(c * S5_SEGMENTS, S5_SEGMENTS), S5_SEGMENTS), lane0:lane0 + 128]

    def scan(init, store):
        def body(c, carry):
            c_rev = seg_len - 1 - c
            new = []
            for g in range(n_g):
                v1f, v2f, v1b, v2b = carry[4 * g:4 * g + 4]
                if store:
                    h_ref[g, pl.ds(pl.multiple_of(c * S5_SEGMENTS, S5_SEGMENTS), S5_SEGMENTS), 0:128] = v1f
                    h_ref[g, pl.ds(pl.multiple_of(c_rev * S5_SEGMENTS, S5_SEGMENTS), S5_SEGMENTS), 128:256] = v1b
                new.extend(_s5_step(v1f, v2f, tile(s_ref, g, c, 0), tile(s_ref, g, c, 128), *cf[g]))
                new.extend(_s5_step(v1b, v2b, tile(s_ref, g, c_rev, 256), tile(s_ref, g, c_rev, 384), *cb[g]))
            return tuple(new)

        return lax.fori_loop(0, seg_len, body, init)

    zero = jnp.zeros((S5_SEGMENTS, 128), F32)
    ends = scan((zero,) * (4 * n_g), False)

    sub = lax.broadcasted_iota(jnp.int32, (S5_SEGMENTS, 128), 0) % segs_per_batch
    first, last = sub == 0, sub == segs_per_batch - 1
    init = []
    for g in range(n_g):
        e1f, e2f, e1b, e2b = ends[4 * g:4 * g + 4]
        pf, pb = rows3(g, 6), rows3(g, 9)
        h1f = h2f = h1b = h2b = zero
        for _ in range(segs_per_batch - 1):
            t1, t2 = _s5_step(h1f, h2f, e1f, e2f, *pf)
            h1f = jnp.where(first, 0.0, pltpu.roll(t1, 1, axis=0))
            h2f = jnp.where(first, 0.0, pltpu.roll(t2, 1, axis=0))
            t1, t2 = _s5_step(h1b, h2b, e1b, e2b, *pb)
            h1b = jnp.where(last, 0.0, pltpu.roll(t1, S5_SEGMENTS - 1, axis=0))
            h2b = jnp.where(last, 0.0, pltpu.roll(t2, S5_SEGMENTS - 1, axis=0))
        init.extend([h1f, h2f, h1b, h2b])
    scan(tuple(init), True)

    for g in range(n_g):
        y_ref[g] = y_ref[g] + _dot(h_ref[g].astype(BF16), v_ref[g])


def _s5_core(u2d, mw, v_mat, coef, bsz, seq):
    m, d = u2d.shape
    n_rows = m // SSM_SUB
    seg_len = n_rows // S5_SEGMENTS
    gb = S5_GROUPS_PER_STEP
    u_g = (u2d.reshape(S5_SEGMENTS, seg_len, SSM_SUB, SSM_GROUPS, SSM_GROUP)
           .transpose(3, 1, 0, 2, 4).reshape(SSM_GROUPS, n_rows, SSM_FLAT))
    y_g = pl.pallas_call(
        functools.partial(_s5_kernel, seg_len=seg_len, segs_per_batch=S5_SEGMENTS // bsz),
        grid=(SSM_GROUPS // gb,),
        in_specs=[
            pl.BlockSpec((gb, n_rows, SSM_FLAT), lambda g: (g, 0, 0)),
            pl.BlockSpec((gb, SSM_FLAT, 3 * SSM_FLAT), lambda g: (g, 0, 0)),
            pl.BlockSpec((gb, SSM_FLAT, SSM_FLAT), lambda g: (g, 0, 0)),
            pl.BlockSpec((gb, 16, 128), lambda g: (g, 0, 0)),
        ],
        out_specs=pl.BlockSpec((gb, n_rows, SSM_FLAT), lambda g: (g, 0, 0)),
        out_shape=jax.ShapeDtypeStruct((SSM_GROUPS, n_rows, SSM_FLAT), F32),
        scratch_shapes=[pltpu.VMEM((gb, n_rows, 2 * SSM_FLAT), F32), pltpu.VMEM((gb, n_rows, SSM_FLAT), F32)],
        compiler_params=_cparams("parallel"),
        name="s5_core",
    )(u_g, mw, v_mat, coef)
    return (y_g.reshape(SSM_GROUPS, seg_len, S5_SEGMENTS, SSM_SUB, SSM_GROUP)
            .transpose(2, 1, 3, 0, 4).reshape(m, d))


def _gelu_tanh(v):
    return 0.5 * v * (1.0 + jnp.tanh(math.sqrt(2.0 / math.pi) * (v + 0.044715 * (v * v * v))))


def _s5_out_kernel(y_ref, wa_ref, wg_ref, x_ref, mod_ref, o_ref, a_ref, *, gate_idx):
    @pl.when(pl.program_id(1) == 0)
    def _():
        a_ref[...] = _gelu_tanh(y_ref[...]).astype(BF16)

    a = a_ref[...]
    val = _dot(a, wa_ref[...])
    glu = _dot(a, wg_ref[...])
    gate = mod_ref[0, gate_idx:gate_idx + 1, :]
    o_ref[...] = x_ref[...] + gate * (val * jax.nn.sigmoid(glu))


def _s5_out(y2d, w_out, x2d, mod, seq, gate_idx):
    m, d = y2d.shape
    tm = _pick(seq, 1024)
    tn = _pick(d, 512)
    nj = d // tn
    blocks_per_seq = seq // tm
    return pl.pallas_call(
        functools.partial(_s5_out_kernel, gate_idx=gate_idx),
        grid=(m // tm, nj),
        in_specs=[
            pl.BlockSpec((tm, d), lambda i, j: (i, 0)),
            pl.BlockSpec((d, tn), lambda i, j: (0, j)),
            pl.BlockSpec((d, tn), lambda i, j: (0, j + nj)),
            pl.BlockSpec((tm, tn), lambda i, j: (i, j)),
            pl.BlockSpec((1, N_MOD, tn), lambda i, j: (i // blocks_per_seq, 0, j)),
        ],
        out_specs=pl.BlockSpec((tm, tn), lambda i, j: (i, j)),
        out_shape=jax.ShapeDtypeStruct((m, d), F32),
        scratch_shapes=[pltpu.VMEM((tm, d), BF16)],
        compiler_params=_cparams("parallel", "arbitrary"),
        name="s5_out",
    )(y2d, w_out, w_out, x2d, mod)


def _s5_mixer(x2d, mod, gain, p, bsz, seq):
    u = _proj(x2d, mod, gain, p['w_in'], seq, 3, BF16)
    y = _s5_core(u, p['mw'], p['v'], p['coef'], bsz, seq)
    return _s5_out(y, p['w_out'], x2d, mod, seq, 5)


def _gla_kernel(q_ref, k_ref, v_ref, glr_ref, wgh_ref, wgl_ref, bg_ref, o_ref, st_ref, lg_ref,
                *, reverse, n_chunks):
    cs = GLA_CHUNK

    @pl.when(pl.program_id(2) == 0)
    def _():
        st_ref[...] = jnp.zeros_like(st_ref)

    a_hi, a_lo = _split_bf16(glr_ref[...])
    w_hi, w_lo = wgh_ref[...], wgl_ref[...]
    gl = _dot(a_hi, w_hi) + _dot(a_lo, w_hi) + _dot(a_hi, w_lo) + bg_ref[...]
    lg_ref[...] = (jnp.minimum(gl, 0.0) - jnp.log(1.0 + jnp.exp(-jnp.abs(gl)))) * (1.0 / GLA_TAU)

    row = lax.broadcasted_iota(jnp.int32, (cs, cs), 0)
    col = lax.broadcasted_iota(jnp.int32, (cs, cs), 1)
    if reverse:
        tri = (col >= row).astype(BF16)
        keep = col > row
    else:
        tri = (col <= row).astype(BF16)
        keep = col <= row
    q_scale = GLA_DKH ** -0.5

    def body(it, carry):
        ci = (n_chunks - 1 - it) if reverse else it
        sl = pl.ds(pl.multiple_of(ci * cs, cs), cs)
        lg_hi, lg_lo = _split_bf16(lg_ref[sl, :])
        bc = _dot(tri, lg_hi) + _dot(tri, lg_lo)
        b_end = bc[0:1, :] if reverse else bc[cs - 1:cs, :]
        qf = q_ref[sl, :].astype(F32) * q_scale
        kf = k_ref[sl, :].astype(F32)
        vv = v_ref[sl, :]
        qd = (qf * jnp.exp(bc)).astype(BF16)
        kd = (kf * jnp.exp(-bc)).astype(BF16)
        kl = (kf * jnp.exp(b_end - bc)).astype(BF16)
        att = jnp.where(keep, _dot_nt(qd, kd), 0.0).astype(BF16)
        st = st_ref[...]
        o_ref[sl, :] = _dot(att, vv) + _dot_nt(qd, st.astype(BF16))
        st_ref[...] = st * jnp.exp(b_end) + _dot_tn(vv, kl)
        return carry

    lax.fori_loop(0, n_chunks, body, 0)


def _gla_direction(proj, glr, wg_hi, wg_lo, bg, bsz, seq, reverse):
    m = proj.shape[0]
    tb = _pick(seq, 512)
    nb = seq // tb
    nq = GLA_DK // GLA_DKH

    def rows(b, h, n):
        return b * nb + ((nb - 1 - n) if reverse else n)

    return pl.pallas_call(
        functools.partial(_gla_kernel, reverse=reverse, n_chunks=tb // GLA_CHUNK),
        grid=(bsz, GLA_HEADS, nb),
        in_specs=[
            pl.BlockSpec((tb, GLA_DKH), lambda b, h, n: (rows(b, h, n), h)),
            pl.BlockSpec((tb, GLA_DKH), lambda b, h, n: (rows(b, h, n), nq + h)),
            pl.BlockSpec((tb, GLA_DVH), lambda b, h, n: (rows(b, h, n), 2 * GLA_DK // GLA_DVH + h)),
            pl.BlockSpec((tb, 128), lambda b, h, n: (rows(b, h, n), 0)),
            pl.BlockSpec((128, GLA_DKH), lambda b, h, n: (0, h)),
            pl.BlockSpec((128, GLA_DKH), lambda b, h, n: (0, h)),
            pl.BlockSpec((1, GLA_DKH), lambda b, h, n: (0, h)),
        ],
        out_specs=pl.BlockSpec((tb, GLA_DVH), lambda b, h, n: (rows(b, h, n), h)),
        out_shape=jax.ShapeDtypeStruct((m, GLA_DV), F32),
        scratch_shapes=[pltpu.VMEM((GLA_DVH, GLA_DKH), F32), pltpu.VMEM((tb, GLA_DKH), F32)],
        compiler_params=_cparams("parallel", "parallel", "arbitrary"),
        name="gla_bwd" if reverse else "gla_fwd",
    )(proj, proj, proj, glr, wg_hi, wg_lo, bg)


def _gla_out_kernel(of_ref, ob_ref, r_ref, ng_ref, w_ref, x_ref, mod_ref, o_ref, a_ref, *, gate_idx):
    @pl.when(pl.program_id(1) == 0)
    def _():
        for h in range(GLA_HEADS):
            sl = slice(h * GLA_DVH, (h + 1) * GLA_DVH)
            o = of_ref[:, sl] + ob_ref[:, sl]
            ms = jnp.mean(o * o, axis=-1, keepdims=True)
            on = (o * lax.rsqrt(ms + NORM_EPS)) * ng_ref[:, sl]
            a_ref[:, sl] = (on * _silu(r_ref[:, sl].astype(F32))).astype(BF16)

    gate = mod_ref[0, gate_idx:gate_idx + 1, :]
    o_ref[...] = x_ref[...] + gate * _dot(a_ref[...], w_ref[...])


def _gla_out(o_f, o_b, proj, norm_g, w_o, x2d, mod, seq, gate_idx):
    m, dv = o_f.shape
    d = w_o.shape[1]
    tm = _pick(seq, 512)
    tn = _pick(d, 512)
    blocks_per_seq = seq // tm
    r_col = (2 * GLA_DK + GLA_DV) // dv
    return pl.pallas_call(
        functools.partial(_gla_out_kernel, gate_idx=gate_idx),
        grid=(m // tm, d // tn),
        in_specs=[
            pl.BlockSpec((tm, dv), lambda i, j: (i, 0)),
            pl.BlockSpec((tm, dv), lambda i, j: (i, 0)),
            pl.BlockSpec((tm, dv), lambda i, j: (i, r_col)),
            pl.BlockSpec((1, dv), lambda i, j: (0, 0)),
            pl.BlockSpec((dv, tn), lambda i, j: (0, j)),
            pl.BlockSpec((tm, tn), lambda i, j: (i, j)),
            pl.BlockSpec((1, N_MOD, tn), lambda i, j: (i // blocks_per_seq, 0, j)),
        ],
        out_specs=pl.BlockSpec((tm, tn), lambda i, j: (i, j)),
        out_shape=jax.ShapeDtypeStruct((m, d), F32),
        scratch_shapes=[pltpu.VMEM((tm, dv), BF16)],
        compiler_params=_cparams("parallel", "arbitrary"),
        name="gla_out",
    )(o_f, o_b, proj, norm_g, w_o, x2d, mod)


def _gla_mixer(x2d, mod, gain, p, bsz, seq):
    proj = _proj(x2d, mod, gain, p['w_main'], seq, 3, BF16)
    glr = _proj(x2d, mod, gain, p['w_code'], seq, 3, F32)
    o_f = _gla_direction(proj, glr, p['wg_hi'][0], p['wg_lo'][0], p['bg'][0], bsz, seq, False)
    o_b = _gla_direction(proj, glr, p['wg_hi'][1], p['wg_lo'][1], p['bg'][1], bsz, seq, True)
    return _gla_out(o_f, o_b, proj, p['norm_g'], p['w_o'], x2d, mod, seq, 5)


def _dsa_proj_kernel(x_ref, mod_ref, g_ref, w_ref, gain_ref, cos_ref, sin_ref, o_ref, h_ref, acc_ref,
                     *, shift_idx):
    j = pl.program_id(1)

    @pl.when(j == 0)
    def _():
        _norm_mod(x_ref, mod_ref, g_ref, h_ref, shift_idx)

    acc_ref[...] = _dot(h_ref[...], w_ref[...])

    @pl.when(j < 2 * DSA_GROUPS)
    def _():
        gain = gain_ref[0]
        cosf, sinf = cos_ref[...], sin_ref[...]
        for h in range(DSA_HEADS):
            sl = slice(h * DSA_HEAD_DIM, (h + 1) * DSA_HEAD_DIM)
            y = acc_ref[:, sl]
            ms = jnp.mean(y * y, axis=-1, keepdims=True)
            yn = (y * lax.rsqrt(ms + NORM_EPS)) * gain
            rot = pltpu.roll(yn, DSA_HEAD_DIM // 2, axis=1)
            o_ref[:, sl] = (yn * cosf + rot * sinf).astype(o_ref.dtype)

    @pl.when(j >= 2 * DSA_GROUPS)
    def _():
        o_ref[...] = acc_ref[...].astype(o_ref.dtype)


def _dsa_proj(x2d, mod, gain, w, qk_gain, cosf, sinf, seq, shift_idx):
    m, d = x2d.shape
    n = w.shape[1]
    tm = _pick(seq, 1024)
    tn = DSA_GW
    blocks_per_seq = seq // tm
    last_qk = 2 * DSA_GROUPS - 1
    return pl.pallas_call(
        functools.partial(_dsa_proj_kernel, shift_idx=shift_idx),
        grid=(m // tm, n // tn),
        in_specs=_nm_specs(tm, d, seq) + [
            pl.BlockSpec((d, tn), lambda i, j: (0, j)),
            pl.BlockSpec((1, 1, DSA_HEAD_DIM), lambda i, j: (jnp.minimum(j, last_qk), 0, 0)),
            pl.BlockSpec((tm, DSA_HEAD_DIM), lambda i, j: (i % blocks_per_seq, 0)),
            pl.BlockSpec((tm, DSA_HEAD_DIM), lambda i, j: (i % blocks_per_seq, 0)),
        ],
        out_specs=pl.BlockSpec((tm, tn), lambda i, j: (i, j)),
        out_shape=jax.ShapeDtypeStruct((m, n), BF16),
        scratch_shapes=[pltpu.VMEM((tm, d), BF16), pltpu.VMEM((tm, tn), F32)],
        compiler_params=_cparams("parallel", "arbitrary"),
        name="dsa_proj",
    )(x2d, mod, gain, w, qk_gain, cosf, sinf)


def _dsa_attn_kernel(q_ref, kc_ref, kl_ref, kr_ref, vc_ref, vl_ref, vr_ref, o_ref, lse_ref, kbuf, vbuf,
                     *, tq, n_rows):
    side, qs = DSA_SIDE, DSA_QSUB
    nq = pl.program_id(2)
    kbuf[0:side, :] = kl_ref[0]
    kbuf[side:side + tq, :] = kc_ref[0]
    kbuf[side + tq:, :] = kr_ref[0]
    vbuf[0:side, :] = vl_ref[0]
    vbuf[side:side + tq, :] = vc_ref[0]
    vbuf[side + tq:, :] = vr_ref[0]

    nk = qs + 2 * side
    qi = lax.broadcasted_iota(jnp.int32, (qs, nk), 0)
    kj = lax.broadcasted_iota(jnp.int32, (qs, nk), 1)
    off = kj - qi
    lane = lax.broadcasted_iota(jnp.int32, (qs, 128), 1)

    def body(i, carry):
        r0 = pl.multiple_of(i * qs, qs)
        kpos = nq * tq + i * qs - side + kj
        ok = (off >= 0) & (off <= 2 * side) & (kpos >= 0) & (kpos < n_rows)
        lse_all = jnp.zeros((qs, 128), F32)
        for h in range(DSA_HEADS):
            sl = slice(h * DSA_HEAD_DIM, (h + 1) * DSA_HEAD_DIM)
            s = _dot_nt(q_ref[0, pl.ds(r0, qs), sl], kbuf[pl.ds(r0, nk), sl])
            s = jnp.where(ok, s, -1e30)
            mx = jnp.max(s, axis=-1, keepdims=True)
            p = jnp.exp(s - mx)
            den = jnp.sum(p, axis=-1, keepdims=True)
            o = _dot(p.astype(BF16), vbuf[pl.ds(r0, nk), sl])
            o_ref[0, pl.ds(r0, qs), sl] = o / den
            lse_all = jnp.where(lane == h, mx + jnp.log(den), lse_all)
        lse_ref[0, pl.ds(r0, qs), :] = lse_all
        return carry

    lax.fori_loop(0, tq // qs, body, 0)


def _dsa_attn(qkv, bsz, seq, group, dil):
    n_rows = seq // dil
    tq = _pick(n_rows, 512)
    nblk = n_rows // tq
    hb = tq // DSA_SIDE
    n_halo = n_rows // DSA_SIDE
    qkv_v = qkv.reshape(bsz, n_rows, dil * DSA_QKV)
    cb = DSA_QKV // DSA_GW
    q_col, k_col, v_col = group, DSA_GROUPS + group, 2 * DSA_GROUPS + group

    def cur(col):
        return pl.BlockSpec((1, tq, DSA_GW), lambda b, r, n: (b, n, r * cb + col))

    def left(col):
        return pl.BlockSpec((1, DSA_SIDE, DSA_GW), lambda b, r, n: (b, jnp.maximum(n * hb - 1, 0), r * cb + col))

    def right(col):
        return pl.BlockSpec((1, DSA_SIDE, DSA_GW),
                            lambda b, r, n: (b, jnp.minimum((n + 1) * hb, n_halo - 1), r * cb + col))

    o, lse = pl.pallas_call(
        functools.partial(_dsa_attn_kernel, tq=tq, n_rows=n_rows),
        grid=(bsz, dil, nblk),
        in_specs=[cur(q_col), cur(k_col), left(k_col), right(k_col), cur(v_col), left(v_col), right(v_col)],
        out_specs=[
            pl.BlockSpec((1, tq, DSA_GW), lambda b, r, n: (b, n, r)),
            pl.BlockSpec((1, tq, 128), lambda b, r, n: (b, n, r)),
        ],
        out_shape=[
            jax.ShapeDtypeStruct((bsz, n_rows, dil * DSA_GW), F32),
            jax.ShapeDtypeStruct((bsz, n_rows, dil * 128), F32),
        ],
        scratch_shapes=[pltpu.VMEM((tq + 2 * DSA_SIDE, DSA_GW), BF16), pltpu.VMEM((tq + 2 * DSA_SIDE, DSA_GW), BF16)],
        compiler_params=_cparams("parallel", "parallel", "arbitrary"),
        name=f"dsa_attn_d{dil}",
    )(qkv_v, qkv_v, qkv_v, qkv_v, qkv_v, qkv_v, qkv_v)
    return o.reshape(bsz * seq, DSA_GW), lse.reshape(bsz * seq, 128)


def _dsa_out_kernel(o0_ref, o1_ref, o2_ref, l0_ref, l1_ref, l2_ref, w_ref, x_ref, mod_ref, o_ref, a_ref,
                    *, gate_idx):
    @pl.when(pl.program_id(1) == 0)
    def _():
        l0, l1, l2 = l0_ref[...], l1_ref[...], l2_ref[...]
        mx = jnp.maximum(jnp.maximum(l0, l1), l2)
        e0, e1, e2 = jnp.exp(l0 - mx), jnp.exp(l1 - mx), jnp.exp(l2 - mx)
        inv = 1.0 / (e0 + e1 + e2)
        w0, w1, w2 = e0 * inv, e1 * inv, e2 * inv
        for h in range(DSA_HEADS):
            sl = slice(h * DSA_HEAD_DIM, (h + 1) * DSA_HEAD_DIM)
            merged = (w0[:, h:h + 1] * o0_ref[:, sl] + w1[:, h:h + 1] * o1_ref[:, sl]
                      + w2[:, h:h + 1] * o2_ref[:, sl])
            a_ref[:, sl] = merged.astype(BF16)

    gate = mod_ref[0, gate_idx:gate_idx + 1, :]
    o_ref[...] = x_ref[...] + gate * _dot(a_ref[...], w_ref[...])


def _dsa_out(outs, lses, w_o, x2d, mod, seq, gate_idx):
    m, dw = outs[0].shape
    d = w_o.shape[1]
    tm = _pick(seq, 512)
    tn = _pick(d, 512)
    blocks_per_seq = seq // tm
    row = lambda width: pl.BlockSpec((tm, width), lambda i, j: (i, 0))
    return pl.pallas_call(
        functools.partial(_dsa_out_kernel, gate_idx=gate_idx),
        grid=(m // tm, d // tn),
        in_specs=[row(dw), row(dw), row(dw), row(128), row(128), row(128),
                  pl.BlockSpec((dw, tn), lambda i, j: (0, j)),
                  pl.BlockSpec((tm, tn), lambda i, j: (i, j)),
                  pl.BlockSpec((1, N_MOD, tn), lambda i, j: (i // blocks_per_seq, 0, j))],
        out_specs=pl.BlockSpec((tm, tn), lambda i, j: (i, j)),
        out_shape=jax.ShapeDtypeStruct((m, d), F32),
        scratch_shapes=[pltpu.VMEM((tm, dw), BF16)],
        compiler_params=_cparams("parallel", "arbitrary"),
        name="dsa_out",
    )(*outs, *lses, w_o, x2d, mod)


def _rope_tables(seq):
    pos = jnp.arange(seq, dtype=F32)
    inv = ROPE_THETA ** (-jnp.arange(0, DSA_HEAD_DIM, 2, dtype=F32) / DSA_HEAD_DIM)
    ang = pos[:, None] * inv[None, :]
    cos, sin = jnp.cos(ang), jnp.sin(ang)
    return jnp.concatenate([cos, cos], axis=-1), jnp.concatenate([-sin, sin], axis=-1)


def _dsa_mixer(x2d, mod, gain, p, bsz, seq):
    cosf, sinf = _rope_tables(seq)
    qkv = _dsa_proj(x2d, mod, gain, p['w_in'], p['qk_gain'], cosf, sinf, seq, 3)
    outs, lses = [], []
    for group, (_, dil) in enumerate(DSA_PATTERNS):
        o, lse = _dsa_attn(qkv, bsz, seq, group, dil)
        outs.append(o)
        lses.append(lse)
    return _dsa_out(outs, lses, p['w_o'], x2d, mod, seq, 5)


def _s5_params(w_in, lam_re, lam_im, log_dt, b_re, b_im, c_re, c_im, d_skip, w_out, bsz, seq):
    seg_len = bsz * seq // (SSM_SUB * S5_SEGMENTS)
    mw, v_mat, coef = _s5_weights(lam_re, lam_im, log_dt, b_re, b_im, c_re, c_im, d_skip, seg_len)
    return dict(w_in=w_in.astype(BF16), w_out=w_out.astype(BF16), mw=mw, v=v_mat, coef=coef)


def _gla_params(w_in, w_gate, b_gate, norm_g, w_o):
    d = w_in.shape[0]
    n_main = 2 * GLA_DK + 2 * GLA_DV
    w_code = jnp.zeros((d, 128), F32).at[:, :2 * GLA_RANK].set(w_in[:, n_main:]).astype(BF16)
    wg = jnp.zeros((2, 128, GLA_DK), F32)
    wg = wg.at[0, :GLA_RANK].set(w_gate[0]).at[1, GLA_RANK:2 * GLA_RANK].set(w_gate[1])
    wg_hi, wg_lo = _split_bf16(wg)
    return dict(w_main=w_in[:, :n_main].astype(BF16), w_code=w_code, wg_hi=wg_hi, wg_lo=wg_lo,
                bg=b_gate.reshape(2, 1, GLA_DK), norm_g=norm_g.reshape(1, GLA_DV), w_o=w_o.astype(BF16))


def _dsa_params(w_in, q_norm, k_norm, w_o):
    qk_gain = jnp.concatenate([q_norm * (DSA_HEAD_DIM ** -0.5), k_norm], axis=0)
    return dict(w_in=w_in.astype(BF16), qk_gain=qk_gain.reshape(2 * DSA_GROUPS, 1, DSA_HEAD_DIM),
                w_o=w_o.astype(BF16))


def kernel(x, c, ada_w, ada_b, norm_g, ffn1_wi, ffn1_wo, ffn2_wi, ffn2_wo, ssm_w_in, ssm_lam_re, ssm_lam_im, ssm_log_dt, ssm_b_re, ssm_b_im, ssm_c_re, ssm_c_im, ssm_d, ssm_w_out, gla_w_in, gla_w_gate, gla_b_gate, gla_norm_g, gla_w_o, dsa_w_in, dsa_q_norm, dsa_k_norm, dsa_w_o):
    bsz, seq, d = x.shape
    depth = ada_w.shape[0]
    mods = _ada(c, ada_w, ada_b)
    x2d = x.reshape(bsz * seq, d)
    for i in range(depth):
        mod = mods[i]
        gains = norm_g[i].reshape(3, 1, d)
        act = _ffn_in(x2d, mod, gains[0], ffn1_wi[i].astype(BF16), seq, 0)
        x2d = _ffn_out(act, ffn1_wo[i].astype(BF16), x2d, mod, seq, 2)

        kind, j = i % 3, i // 3
        if kind == 0:
            p = _s5_params(ssm_w_in[j], ssm_lam_re[j], ssm_lam_im[j], ssm_log_dt[j], ssm_b_re[j], ssm_b_im[j],
                           ssm_c_re[j], ssm_c_im[j], ssm_d[j], ssm_w_out[j], bsz, seq)
            x2d = _s5_mixer(x2d, mod, gains[1], p, bsz, seq)
        elif kind == 1:
            p = _gla_params(gla_w_in[j], gla_w_gate[j], gla_b_gate[j], gla_norm_g[j], gla_w_o[j])
            x2d = _gla_mixer(x2d, mod, gains[1], p, bsz, seq)
        else:
            p = _dsa_params(dsa_w_in[j], dsa_q_norm[j], dsa_k_norm[j], dsa_w_o[j])
            x2d = _dsa_mixer(x2d, mod, gains[1], p, bsz, seq)

        act = _ffn_in(x2d, mod, gains[2], ffn2_wi[i].astype(BF16), seq, 6)
        x2d = _ffn_out(act, ffn2_wo[i].astype(BF16), x2d, mod, seq, 8)
    return x2d.reshape(bsz, seq, d)
```

```python
import functools
import math

import jax
import jax.numpy as jnp
from jax import lax
from jax.experimental import pallas as pl
from jax.experimental.pallas import tpu as pltpu

F32 = jnp.float32
BF16 = jnp.bfloat16

D_MODEL = 2048
D_FF = 5632
N_MOD = 9
NORM_EPS = 1e-6

SSM_GROUP = 16
SSM_GROUPS = D_MODEL // SSM_GROUP
SSM_STATE = 64
SSM_SUB = 16
SSM_FLAT = SSM_SUB * SSM_GROUP

GLA_HEADS = 4
GLA_DK = D_MODEL // 2
GLA_DV = D_MODEL
GLA_DKH = GLA_DK // GLA_HEADS
GLA_DVH = GLA_DV // GLA_HEADS
GLA_RANK = 16
GLA_TAU = 16.0
GLA_CHUNK = 64

DSA_PATTERNS = ((128, 1), (512, 4), (2048, 16))
DSA_GROUPS = 3
DSA_HEADS = 8
DSA_HEAD_DIM = 128
DSA_GW = DSA_HEADS * DSA_HEAD_DIM
DSA_QKV = 3 * DSA_GROUPS * DSA_GW
DSA_SIDE = 64
DSA_QSUB = 128
ROPE_THETA = 10000.0

V7X_VMEM_LIMIT = 56 * 1024 * 1024


def _cparams(*sem):
    return pltpu.CompilerParams(dimension_semantics=sem, vmem_limit_bytes=V7X_VMEM_LIMIT)


def _pick(n, pref):
    t = min(n, pref)
    while n % t:
        t //= 2
    return t


def _silu(v):
    return v * jax.nn.sigmoid(v)


def _dot(a, b):
    return jnp.dot(a, b, preferred_element_type=F32)


def _dot_nt(a, b):
    return lax.dot_general(a, b, (((1,), (1,)), ((), ())), preferred_element_type=F32)


def _dot_tn(a, b):
    return lax.dot_general(a, b, (((0,), (0,)), ((), ())), preferred_element_type=F32)


def _split_bf16(v):
    hi = v.astype(BF16)
    lo = (v - hi.astype(F32)).astype(BF16)
    return hi, lo


def _ada_kernel(c_ref, w_ref, b_ref, o_ref):
    hi, lo = _split_bf16(_silu(c_ref[...]))
    w = w_ref[0].astype(BF16)
    o_ref[0] = _dot(hi, w) + _dot(lo, w) + b_ref[0]


def _ada(c, ada_w, ada_b):
    depth, d, n = ada_w.shape
    bsz = c.shape[0]
    rows = 8
    c_pad = jnp.zeros((rows, d), F32).at[:bsz].set(c)
    tn = _pick(n, 1024)
    out = pl.pallas_call(
        _ada_kernel,
        grid=(depth, n // tn),
        in_specs=[
            pl.BlockSpec((rows, d), lambda i, j: (0, 0)),
            pl.BlockSpec((1, d, tn), lambda i, j: (i, 0, j)),
            pl.BlockSpec((1, 1, tn), lambda i, j: (i, 0, j)),
        ],
        out_specs=pl.BlockSpec((1, rows, tn), lambda i, j: (i, 0, j)),
        out_shape=jax.ShapeDtypeStruct((depth, rows, n), F32),
        compiler_params=_cparams("parallel", "parallel"),
        name="ada_mod",
    )(c_pad, ada_w, ada_b.reshape(depth, 1, n))
    return out[:, :bsz].reshape(depth, bsz, N_MOD, d)


def _norm_mod(x_ref, mod_ref, g_ref, h_ref, shift_idx):
    tm = x_ref.shape[0]
    rows = _pick(tm, 256)
    shift = mod_ref[0, shift_idx:shift_idx + 1, :]
    scale1 = 1.0 + mod_ref[0, shift_idx + 1:shift_idx + 2, :]
    g = g_ref[...]

    def body(r, carry):
        sl = pl.ds(pl.multiple_of(r * rows, rows), rows)
        x = x_ref[sl, :]
        ms = jnp.mean(x * x, axis=-1, keepdims=True)
        h = (x * lax.rsqrt(ms + NORM_EPS)) * g
        h_ref[sl, :] = (h * scale1 + shift).astype(BF16)
        return carry

    lax.fori_loop(0, tm // rows, body, 0)


def _nm_specs(tm, d, seq):
    blocks_per_seq = seq // tm
    return [
        pl.BlockSpec((tm, d), lambda i, j: (i, 0)),
        pl.BlockSpec((1, N_MOD, d), lambda i, j: (i // blocks_per_seq, 0, 0)),
        pl.BlockSpec((1, d), lambda i, j: (0, 0)),
    ]


def _ffn_in_kernel(x_ref, mod_ref, g_ref, wg_ref, wu_ref, o_ref, h_ref, *, shift_idx):
    @pl.when(pl.program_id(1) == 0)
    def _():
        _norm_mod(x_ref, mod_ref, g_ref, h_ref, shift_idx)

    h = h_ref[...]
    a = _dot(h, wg_ref[...])
    u = _dot(h, wu_ref[...])
    o_ref[...] = (_silu(a) * u).astype(o_ref.dtype)


def _ffn_in(x2d, mod, gain, wi, seq, shift_idx):
    m, d = x2d.shape
    f = wi.shape[1] // 2
    tm = _pick(seq, 1024)
    tf = _pick(f, 512)
    nf = f // tf
    return pl.pallas_call(
        functools.partial(_ffn_in_kernel, shift_idx=shift_idx),
        grid=(m // tm, nf),
        in_specs=_nm_specs(tm, d, seq) + [
            pl.BlockSpec((d, tf), lambda i, j: (0, j)),
            pl.BlockSpec((d, tf), lambda i, j: (0, j + nf)),
        ],
        out_specs=pl.BlockSpec((tm, tf), lambda i, j: (i, j)),
        out_shape=jax.ShapeDtypeStruct((m, f), BF16),
        scratch_shapes=[pltpu.VMEM((tm, d), BF16)],
        compiler_params=_cparams("parallel", "arbitrary"),
        name="ffn_in",
    )(x2d, mod, gain, wi, wi)


def _ffn_out_kernel(a_ref, w_ref, x_ref, mod_ref, o_ref, *, gate_idx):
    y = _dot(a_ref[...], w_ref[...])
    gate = mod_ref[0, gate_idx:gate_idx + 1, :]
    o_ref[...] = x_ref[...] + (0.5 * gate) * y


def _ffn_out(act, wo, x2d, mod, seq, gate_idx):
    m, f = act.shape
    d = wo.shape[1]
    tm = _pick(seq, 1024)
    tn = _pick(d, 512)
    blocks_per_seq = seq // tm
    return pl.pallas_call(
        functools.partial(_ffn_out_kernel, gate_idx=gate_idx),
        grid=(m // tm, d // tn),
        in_specs=[
            pl.BlockSpec((tm, f), lambda i, j: (i, 0)),
            pl.BlockSpec((f, tn), lambda i, j: (0, j)),
            pl.BlockSpec((tm, tn), lambda i, j: (i, j)),
            pl.BlockSpec((1, N_MOD, tn), lambda i, j: (i // blocks_per_seq, 0, j)),
        ],
        out_specs=pl.BlockSpec((tm, tn), lambda i, j: (i, j)),
        out_shape=jax.ShapeDtypeStruct((m, d), F32),
        compiler_params=_cparams("parallel", "parallel"),
        name="ffn_out",
    )(act, wo, x2d, mod)


def _proj_kernel(x_ref, mod_ref, g_ref, w_ref, o_ref, h_ref, *, shift_idx):
    @pl.when(pl.program_id(1) == 0)
    def _():
        _norm_mod(x_ref, mod_ref, g_ref, h_ref, shift_idx)

    o_ref[...] = _dot(h_ref[...], w_ref[...]).astype(o_ref.dtype)


def _proj(x2d, mod, gain, w, seq, shift_idx, out_dtype):
    m, d = x2d.shape
    n = w.shape[1]
    tm = _pick(seq, 1024)
    tn = _pick(n, 1024)
    return pl.pallas_call(
        functools.partial(_proj_kernel, shift_idx=shift_idx),
        grid=(m // tm, n // tn),
        in_specs=_nm_specs(tm, d, seq) + [pl.BlockSpec((d, tn), lambda i, j: (0, j))],
        out_specs=pl.BlockSpec((tm, tn), lambda i, j: (i, j)),
        out_shape=jax.ShapeDtypeStruct((m, n), out_dtype),
        scratch_shapes=[pltpu.VMEM((tm, d), BF16)],
        compiler_params=_cparams("parallel", "arbitrary"),
        name="proj",
    )(x2d, mod, gain, w)


def _s5_weights(lam_re, lam_im, log_dt, b_re, b_im, c_re, c_im, d_skip, seg_len):
    t_sub, g_n, p_n, k_n = SSM_SUB, SSM_GROUPS, SSM_STATE, SSM_GROUP
    lr, li = lam_re.astype(F32), lam_im.astype(F32)
    dt = jnp.exp(log_dt.astype(F32))[..., None]
    mag = jnp.exp(lr * dt)
    ar, ai = mag * jnp.cos(li * dt), mag * jnp.sin(li * dt)
    den = lr * lr + li * li
    fr = ((ar - 1.0) * lr + ai * li) / den
    fi = (ai * lr - (ar - 1.0) * li) / den
    bbr = fr[..., None] * b_re - fi[..., None] * b_im
    bbi = fr[..., None] * b_im + fi[..., None] * b_re
    n = jnp.arange(t_sub + 1, dtype=F32)[:, None, None, None]
    pmag = jnp.exp(n * (lr * dt))
    pr, pi = pmag * jnp.cos(n * (li * dt)), pmag * jnp.sin(n * (li * dt))
    cr, ci = c_re.astype(F32), c_im.astype(F32)

    prt, pit = pr[:t_sub, :, :, None, :], pi[:t_sub, :, :, None, :]
    clr = cr[None] * prt - ci[None] * pit
    cli = cr[None] * pit + ci[None] * prt
    bbr_t, bbi_t = bbr.transpose(0, 1, 3, 2), bbi.transpose(0, 1, 3, 2)
    taps = jnp.sum(clr[:, :, :, :, None, :] * bbr_t[None, :, :, None, :, :]
                   - cli[:, :, :, :, None, :] * bbi_t[None, :, :, None, :, :], axis=-1)
    strip_f = taps[:, 0].transpose(1, 3, 0, 2).reshape(g_n, k_n, SSM_FLAT)
    strip_b = taps[::-1, 1].transpose(1, 3, 0, 2).reshape(g_n, k_n, SSM_FLAT)
    rows_f, rows_b = [], []
    for s in range(t_sub):
        lo, hi = s * k_n, (t_sub - 1 - s) * k_n
        rows_f.append(jnp.pad(strip_f[:, :, :SSM_FLAT - lo], ((0, 0), (0, 0), (lo, 0))))
        rows_b.append(jnp.pad(strip_b[:, :, hi:], ((0, 0), (0, 0), (0, hi))))
    m_mat = (jnp.stack(rows_f, axis=1) + jnp.stack(rows_b, axis=1)).reshape(g_n, SSM_FLAT, SSM_FLAT)
    skip = jnp.tile(d_skip.astype(F32).reshape(g_n, 1, k_n), (1, t_sub, 1)).reshape(g_n, SSM_FLAT)
    m_mat = m_mat + jnp.eye(SSM_FLAT, dtype=F32)[None] * skip[:, None, :]

    def inc(pw_r, pw_i, d):
        wr = pw_r[..., None] * bbr[d][None] - pw_i[..., None] * bbi[d][None]
        wi = pw_r[..., None] * bbi[d][None] + pw_i[..., None] * bbr[d][None]
        to = lambda a: a.transpose(1, 0, 3, 2).reshape(g_n, SSM_FLAT, p_n)
        return to(wr), to(wi)

    wfr, wfi = inc(pr[:t_sub, 0][::-1], pi[:t_sub, 0][::-1], 0)
    wbr, wbi = inc(pr[:t_sub, 1], pi[:t_sub, 1], 1)
    w_mat = jnp.concatenate([wfr, wfi, wfi, wfr, wbr, wbi, wbi, wbr], axis=-1)

    def outw(pw_r, pw_i, d):
        er = cr[d][None] * pw_r[:, :, None, :] - ci[d][None] * pw_i[:, :, None, :]
        ei = cr[d][None] * pw_i[:, :, None, :] + ci[d][None] * pw_r[:, :, None, :]
        to = lambda a: a.transpose(1, 3, 0, 2).reshape(g_n, p_n, SSM_FLAT)
        return to(er), to(-ei)

    vfr, vfi = outw(pr[1:, 0], pi[1:, 0], 0)
    vbr, vbi = outw(pr[1:, 1][::-1], pi[1:, 1][::-1], 1)
    v_mat = jnp.concatenate([vfr, vfi, vbr, vbi], axis=1)

    cat = lambda a, b: jnp.concatenate([a, b], axis=-1)

    def step_rows(a_r, a_i):
        return [cat(a_r, a_r), cat(-a_i, a_i), cat(a_i, -a_i)]

    span = float(t_sub * seg_len)
    smag = jnp.exp(span * (lr * dt))
    sr, si = smag * jnp.cos(span * (li * dt)), smag * jnp.sin(span * (li * dt))
    zero = jnp.zeros((g_n, 2 * p_n), F32)
    coef = jnp.stack(step_rows(pr[t_sub, 0], pi[t_sub, 0]) + step_rows(pr[t_sub, 1], pi[t_sub, 1])
                     + step_rows(sr[0], si[0]) + step_rows(sr[1], si[1]) + [zero] * 4, axis=1)
    mw = jnp.concatenate([m_mat, w_mat], axis=-1).astype(BF16)
    return mw, v_mat.astype(BF16), coef


S5_SEGMENTS = 8
S5_LANE_GROUPS = 128 // SSM_GROUP
S5_SCAN_GROUPS = 2
S5_SHUFFLE_ROWS = 64


def _s5_step(v1, v2, s1, s2, c1, c2, c3):
    return c1 * v1 + c2 * v2 + s1, c1 * v2 + c3 * v1 + s2


def _s5_kernel(u_ref, mw_ref, v_ref, coef_ref, y_ref, lhs_ref, yg_ref, s_ref, h_ref, *, n_sub):
    t_sub, n_lg, seg = SSM_SUB, S5_LANE_GROUPS, S5_SEGMENTS
    seg_len = n_sub // seg
    ch = S5_SHUFFLE_ROWS
    blk = lax.broadcasted_iota(jnp.int32, (ch, 128), 1) // SSM_GROUP

    def shuffle_in(i, carry):
        c0 = pl.multiple_of(i * ch, ch)
        a = [u_ref[pl.ds(c0 * t_sub + t, ch, stride=t_sub), :] for t in range(t_sub)]
        for gl in range(n_lg):
            for half in range(2):
                acc = jnp.zeros((ch, 128), F32)
                for tl in range(n_lg):
                    shift = ((tl - gl) % n_lg) * SSM_GROUP
                    src = a[half * n_lg + tl]
                    acc = jnp.where(blk == tl, pltpu.roll(src, shift, axis=1) if shift else src, acc)
                lhs_ref[gl, pl.ds(c0, ch), half * 128:(half + 1) * 128] = acc.astype(BF16)
        return carry

    lax.fori_loop(0, n_sub // ch, shuffle_in, 0)

    def tile(ref, k, c):
        return ref[k, pl.ds(c, seg, stride=seg_len), :]

    sub = lax.broadcasted_iota(jnp.int32, (seg, 128), 0)
    first, last = sub == 0, sub == seg - 1
    zero = jnp.zeros((seg, 128), F32)
    n_sg = S5_SCAN_GROUPS
    for g0 in range(0, n_lg, n_sg):
        for g in range(n_sg):
            u_g = lhs_ref[g0 + g]
            yg_ref[g0 + g] = _dot(u_g, mw_ref[g0 + g, :, 0:SSM_FLAT])
            inc = _dot(u_g, mw_ref[g0 + g, :, SSM_FLAT:])
            for k in range(4):
                s_ref[4 * g + k] = inc[:, k * 128:(k + 1) * 128]

        def rows3(g, r):
            return [jnp.broadcast_to(coef_ref[g0 + g, r + i:r + i + 1, :], (seg, 128)) for i in range(3)]

        cf = [rows3(g, 0) for g in range(n_sg)]
        cb = [rows3(g, 3) for g in range(n_sg)]

        def scan(init, store):
            def body(c, carry):
                c_rev = seg_len - 1 - c
                new = []
                for g in range(n_sg):
                    v1f, v2f, v1b, v2b = carry[4 * g:4 * g + 4]
                    if store:
                        h_ref[2 * g, pl.ds(c, seg, stride=seg_len), :] = v1f
                        h_ref[2 * g + 1, pl.ds(c_rev, seg, stride=seg_len), :] = v1b
                    new.extend(_s5_step(v1f, v2f, tile(s_ref, 4 * g, c), tile(s_ref, 4 * g + 1, c), *cf[g]))
                    new.extend(_s5_step(v1b, v2b, tile(s_ref, 4 * g + 2, c_rev), tile(s_ref, 4 * g + 3, c_rev),
                                        *cb[g]))
                return tuple(new)

            return lax.fori_loop(0, seg_len, body, init)

        ends = scan((zero,) * (4 * n_sg), False)
        init = []
        for g in range(n_sg):
            e1f, e2f, e1b, e2b = ends[4 * g:4 * g + 4]
            pf, pb = rows3(g, 6), rows3(g, 9)
            h1f = h2f = h1b = h2b = zero
            for _ in range(seg - 1):
                t1, t2 = _s5_step(h1f, h2f, e1f, e2f, *pf)
                h1f = jnp.where(first, 0.0, pltpu.roll(t1, 1, axis=0))
                h2f = jnp.where(first, 0.0, pltpu.roll(t2, 1, axis=0))
                t1, t2 = _s5_step(h1b, h2b, e1b, e2b, *pb)
                h1b = jnp.where(last, 0.0, pltpu.roll(t1, seg - 1, axis=0))
                h2b = jnp.where(last, 0.0, pltpu.roll(t2, seg - 1, axis=0))
            init.extend([h1f, h2f, h1b, h2b])
        scan(tuple(init), True)

        for g in range(n_sg):
            h_cat = jnp.concatenate([h_ref[2 * g], h_ref[2 * g + 1]], axis=1).astype(BF16)
            yg_ref[g0 + g] = yg_ref[g0 + g] + _dot(h_cat, v_ref[g0 + g])

    def shuffle_out(i, carry):
        c0 = pl.multiple_of(i * ch, ch)
        for half in range(2):
            yh = [yg_ref[gl, pl.ds(c0, ch), half * 128:(half + 1) * 128] for gl in range(n_lg)]
            for tl in range(n_lg):
                acc = jnp.zeros((ch, 128), F32)
                for gl in range(n_lg):
                    shift = ((gl - tl) % n_lg) * SSM_GROUP
                    acc = jnp.where(blk == gl, pltpu.roll(yh[gl], shift, axis=1) if shift else yh[gl], acc)
                y_ref[pl.ds(c0 * t_sub + half * n_lg + tl, ch, stride=t_sub), :] = acc
        return carry

    lax.fori_loop(0, n_sub // ch, shuffle_out, 0)


def _s5_core(u2d, mw, v_mat, coef, bsz, seq):
    m, d = u2d.shape
    n_sub = seq // SSM_SUB
    n_lg = S5_LANE_GROUPS
    once = pl.Buffered(1)
    return pl.pallas_call(
        functools.partial(_s5_kernel, n_sub=n_sub),
        grid=(d // 128, bsz),
        in_specs=[
            pl.BlockSpec((seq, 128), lambda gb, b: (b, gb), pipeline_mode=once),
            pl.BlockSpec((n_lg, SSM_FLAT, 3 * SSM_FLAT), lambda gb, b: (gb, 0, 0)),
            pl.BlockSpec((n_lg, SSM_FLAT, SSM_FLAT), lambda gb, b: (gb, 0, 0)),
            pl.BlockSpec((n_lg, 16, 128), lambda gb, b: (gb, 0, 0)),
        ],
        out_specs=pl.BlockSpec((seq, 128), lambda gb, b: (b, gb), pipeline_mode=once),
        out_shape=jax.ShapeDtypeStruct((m, d), F32),
        scratch_shapes=[
            pltpu.VMEM((n_lg, n_sub, SSM_FLAT), BF16),
            pltpu.VMEM((n_lg, n_sub, SSM_FLAT), F32),
            pltpu.VMEM((4 * S5_SCAN_GROUPS, n_sub, 128), F32),
            pltpu.VMEM((2 * S5_SCAN_GROUPS, n_sub, 128), F32),
        ],
        compiler_params=_cparams("parallel", "parallel"),
        name="s5_core",
    )(u2d, mw, v_mat, coef)


def _gelu_tanh(v):
    return 0.5 * v * (1.0 + jnp.tanh(math.sqrt(2.0 / math.pi) * (v + 0.044715 * (v * v * v))))


def _s5_out_kernel(y_ref, wa_ref, wg_ref, x_ref, mod_ref, o_ref, a_ref, *, gate_idx):
    @pl.when(pl.program_id(1) == 0)
    def _():
        a_ref[...] = _gelu_tanh(y_ref[...]).astype(BF16)

    a = a_ref[...]
    val = _dot(a, wa_ref[...])
    glu = _dot(a, wg_ref[...])
    gate = mod_ref[0, gate_idx:gate_idx + 1, :]
    o_ref[...] = x_ref[...] + gate * (val * jax.nn.sigmoid(glu))


def _s5_out(y2d, w_out, x2d, mod, seq, gate_idx):
    m, d = y2d.shape
    tm = _pick(seq, 1024)
    tn = _pick(d, 512)
    nj = d // tn
    blocks_per_seq = seq // tm
    return pl.pallas_call(
        functools.partial(_s5_out_kernel, gate_idx=gate_idx),
        grid=(m // tm, nj),
        in_specs=[
            pl.BlockSpec((tm, d), lambda i, j: (i, 0)),
            pl.BlockSpec((d, tn), lambda i, j: (0, j)),
            pl.BlockSpec((d, tn), lambda i, j: (0, j + nj)),
            pl.BlockSpec((tm, tn), lambda i, j: (i, j)),
            pl.BlockSpec((1, N_MOD, tn), lambda i, j: (i // blocks_per_seq, 0, j)),
        ],
        out_specs=pl.BlockSpec((tm, tn), lambda i, j: (i, j)),
        out_shape=jax.ShapeDtypeStruct((m, d), F32),
        scratch_shapes=[pltpu.VMEM((tm, d), BF16)],
        compiler_params=_cparams("parallel", "arbitrary"),
        name="s5_out",
    )(y2d, w_out, w_out, x2d, mod)


def _s5_mixer(x2d, mod, gain, p, bsz, seq):
    u = _proj(x2d, mod, gain, p['w_in'], seq, 3, F32)
    y = _s5_core(u, p['mw'], p['v'], p['coef'], bsz, seq)
    return _s5_out(y, p['w_out'], x2d, mod, seq, 5)


def _gla_kernel(q_ref, k_ref, v_ref, glr_ref, wgh_ref, wgl_ref, bg_ref, o_ref, st_ref, lg_ref,
                *, reverse, n_chunks):
    cs = GLA_CHUNK

    @pl.when(pl.program_id(2) == 0)
    def _():
        st_ref[...] = jnp.zeros_like(st_ref)

    a_hi, a_lo = _split_bf16(glr_ref[...])
    w_hi, w_lo = wgh_ref[...], wgl_ref[...]
    gl = _dot(a_hi, w_hi) + _dot(a_lo, w_hi) + _dot(a_hi, w_lo) + bg_ref[...]
    lg_ref[...] = (jnp.minimum(gl, 0.0) - jnp.log(1.0 + jnp.exp(-jnp.abs(gl)))) * (1.0 / GLA_TAU)

    row = lax.broadcasted_iota(jnp.int32, (cs, cs), 0)
    col = lax.broadcasted_iota(jnp.int32, (cs, cs), 1)
    if reverse:
        tri = (col >= row).astype(BF16)
        keep = col > row
    else:
        tri = (col <= row).astype(BF16)
        keep = col <= row
    q_scale = GLA_DKH ** -0.5

    def body(it, carry):
        ci = (n_chunks - 1 - it) if reverse else it
        sl = pl.ds(pl.multiple_of(ci * cs, cs), cs)
        lg_hi, lg_lo = _split_bf16(lg_ref[sl, :])
        bc = _dot(tri, lg_hi) + _dot(tri, lg_lo)
        b_end = bc[0:1, :] if reverse else bc[cs - 1:cs, :]
        qf = q_ref[sl, :].astype(F32) * q_scale
        kf = k_ref[sl, :].astype(F32)
        vv = v_ref[sl, :]
        qd = (qf * jnp.exp(bc)).astype(BF16)
        kd = (kf * jnp.exp(-bc)).astype(BF16)
        kl = (kf * jnp.exp(b_end - bc)).astype(BF16)
        att = jnp.where(keep, _dot_nt(qd, kd), 0.0).astype(BF16)
        st = st_ref[...]
        o_ref[sl, :] = _dot(att, vv) + _dot_nt(qd, st.astype(BF16))
        st_ref[...] = st * jnp.exp(b_end) + _dot_tn(vv, kl)
        return carry

    lax.fori_loop(0, n_chunks, body, 0)


def _gla_direction(proj, glr, wg_hi, wg_lo, bg, bsz, seq, reverse):
    m = proj.shape[0]
    tb = _pick(seq, 512)
    nb = seq // tb
    nq = GLA_DK // GLA_DKH

    def rows(b, h, n):
        return b * nb + ((nb - 1 - n) if reverse else n)

    return pl.pallas_call(
        functools.partial(_gla_kernel, reverse=reverse, n_chunks=tb // GLA_CHUNK),
        grid=(bsz, GLA_HEADS, nb),
        in_specs=[
            pl.BlockSpec((tb, GLA_DKH), lambda b, h, n: (rows(b, h, n), h)),
            pl.BlockSpec((tb, GLA_DKH), lambda b, h, n: (rows(b, h, n), nq + h)),
            pl.BlockSpec((tb, GLA_DVH), lambda b, h, n: (rows(b, h, n), 2 * GLA_DK // GLA_DVH + h)),
            pl.BlockSpec((tb, 128), lambda b, h, n: (rows(b, h, n), 0)),
            pl.BlockSpec((128, GLA_DKH), lambda b, h, n: (0, h)),
            pl.BlockSpec((128, GLA_DKH), lambda b, h, n: (0, h)),
            pl.BlockSpec((1, GLA_DKH), lambda b, h, n: (0, h)),
        ],
        out_specs=pl.BlockSpec((tb, GLA_DVH), lambda b, h, n: (rows(b, h, n), h)),
        out_shape=jax.ShapeDtypeStruct((m, GLA_DV), F32),
        scratch_shapes=[pltpu.VMEM((GLA_DVH, GLA_DKH), F32), pltpu.VMEM((tb, GLA_DKH), F32)],
        compiler_params=_cparams("parallel", "parallel", "arbitrary"),
        name="gla_bwd" if reverse else "gla_fwd",
    )(proj, proj, proj, glr, wg_hi, wg_lo, bg)


def _gla_out_kernel(of_ref, ob_ref, r_ref, ng_ref, w_ref, x_ref, mod_ref, o_ref, a_ref, *, gate_idx):
    @pl.when(pl.program_id(1) == 0)
    def _():
        for h in range(GLA_HEADS):
            sl = slice(h * GLA_DVH, (h + 1) * GLA_DVH)
            o = of_ref[:, sl] + ob_ref[:, sl]
            ms = jnp.mean(o * o, axis=-1, keepdims=True)
            on = (o * lax.rsqrt(ms + NORM_EPS)) * ng_ref[:, sl]
            a_ref[:, sl] = (on * _silu(r_ref[:, sl].astype(F32))).astype(BF16)

    gate = mod_ref[0, gate_idx:gate_idx + 1, :]
    o_ref[...] = x_ref[...] + gate * _dot(a_ref[...], w_ref[...])


def _gla_out(o_f, o_b, proj, norm_g, w_o, x2d, mod, seq, gate_idx):
    m, dv = o_f.shape
    d = w_o.shape[1]
    tm = _pick(seq, 512)
    tn = _pick(d, 512)
    blocks_per_seq = seq // tm
    r_col = (2 * GLA_DK + GLA_DV) // dv
    return pl.pallas_call(
        functools.partial(_gla_out_kernel, gate_idx=gate_idx),
        grid=(m // tm, d // tn),
        in_specs=[
            pl.BlockSpec((tm, dv), lambda i, j: (i, 0)),
            pl.BlockSpec((tm, dv), lambda i, j: (i, 0)),
            pl.BlockSpec((tm, dv), lambda i, j: (i, r_col)),
            pl.BlockSpec((1, dv), lambda i, j: (0, 0)),
            pl.BlockSpec((dv, tn), lambda i, j: (0, j)),
            pl.BlockSpec((tm, tn), lambda i, j: (i, j)),
            pl.BlockSpec((1, N_MOD, tn), lambda i, j: (i // blocks_per_seq, 0, j)),
        ],
        out_specs=pl.BlockSpec((tm, tn), lambda i, j: (i, j)),
        out_shape=jax.ShapeDtypeStruct((m, d), F32),
        scratch_shapes=[pltpu.VMEM((tm, dv), BF16)],
        compiler_params=_cparams("parallel", "arbitrary"),
        name="gla_out",
    )(o_f, o_b, proj, norm_g, w_o, x2d, mod)


def _gla_mixer(x2d, mod, gain, p, bsz, seq):
    proj = _proj(x2d, mod, gain, p['w_main'], seq, 3, BF16)
    glr = _proj(x2d, mod, gain, p['w_code'], seq, 3, F32)
    o_f = _gla_direction(proj, glr, p['wg_hi'][0], p['wg_lo'][0], p['bg'][0], bsz, seq, False)
    o_b = _gla_direction(proj, glr, p['wg_hi'][1], p['wg_lo'][1], p['bg'][1], bsz, seq, True)
    return _gla_out(o_f, o_b, proj, p['norm_g'], p['w_o'], x2d, mod, seq, 5)


def _dsa_proj_kernel(x_ref, mod_ref, g_ref, w_ref, gain_ref, cos_ref, sin_ref, o_ref, h_ref, res_ref,
                     *, shift_idx, dil):
    j = pl.program_id(1)

    @pl.when(j == 0)
    def _():
        _norm_mod(x_ref, mod_ref, g_ref, h_ref, shift_idx)

    acc = _dot(h_ref[...], w_ref[...])
    for h in range(DSA_HEADS):
        res_ref[h] = acc[:, h * DSA_HEAD_DIM:(h + 1) * DSA_HEAD_DIM]

    @pl.when(j < 2)
    def _():
        gain = gain_ref[0]
        cosf, sinf = cos_ref[...], sin_ref[...]
        for h in range(DSA_HEADS):
            y = res_ref[h]
            ms = jnp.mean(y * y, axis=-1, keepdims=True)
            yn = (y * lax.rsqrt(ms + NORM_EPS)) * gain
            rot = pltpu.roll(yn, DSA_HEAD_DIM // 2, axis=1)
            res_ref[h] = yn * cosf + rot * sinf

    rows = res_ref.shape[1] // dil
    for r in range(dil):
        for h in range(DSA_HEADS):
            sl = slice(h * DSA_HEAD_DIM, (h + 1) * DSA_HEAD_DIM)
            picked = res_ref[h] if dil == 1 else res_ref[h, pl.ds(r, rows, stride=dil), :]
            o_ref[0, r, :, sl] = picked.astype(o_ref.dtype)


def _dsa_proj(x2d, mod, gain, w, qk_gain, cosf, sinf, bsz, seq, shift_idx, dil):
    m, d = x2d.shape
    tm = _pick(seq, 1024)
    tn = DSA_GW
    blocks_per_seq = seq // tm
    return pl.pallas_call(
        functools.partial(_dsa_proj_kernel, shift_idx=shift_idx, dil=dil),
        grid=(m // tm, 3),
        in_specs=_nm_specs(tm, d, seq) + [
            pl.BlockSpec((d, tn), lambda i, j: (0, j)),
            pl.BlockSpec((1, 1, DSA_HEAD_DIM), lambda i, j: (jnp.minimum(j, 1), 0, 0)),
            pl.BlockSpec((tm, DSA_HEAD_DIM), lambda i, j: (i % blocks_per_seq, 0)),
            pl.BlockSpec((tm, DSA_HEAD_DIM), lambda i, j: (i % blocks_per_seq, 0)),
        ],
        out_specs=pl.BlockSpec((1, dil, tm // dil, tn),
                               lambda i, j: (i // blocks_per_seq, 0, i % blocks_per_seq, j)),
        out_shape=jax.ShapeDtypeStruct((bsz, dil, seq // dil, 3 * tn), BF16),
        scratch_shapes=[pltpu.VMEM((tm, d), BF16), pltpu.VMEM((DSA_HEADS, tm, DSA_HEAD_DIM), F32)],
        compiler_params=_cparams("parallel", "arbitrary"),
        name=f"dsa_proj_d{dil}",
    )(x2d, mod, gain, w, qk_gain, cosf, sinf)


def _dsa_attn_kernel(q_ref, kc_ref, kl_ref, kr_ref, vc_ref, vl_ref, vr_ref, o_ref, lse_ref, kbuf, vbuf,
                     *, tq, n_rows):
    side, qs = DSA_SIDE, DSA_QSUB
    nq = pl.program_id(2)
    kbuf[0:side, :] = kl_ref[0, 0]
    kbuf[side:side + tq, :] = kc_ref[0, 0]
    kbuf[side + tq:, :] = kr_ref[0, 0]
    vbuf[0:side, :] = vl_ref[0, 0]
    vbuf[side:side + tq, :] = vc_ref[0, 0]
    vbuf[side + tq:, :] = vr_ref[0, 0]

    nk = qs + 2 * side
    qi = lax.broadcasted_iota(jnp.int32, (qs, nk), 0)
    kj = lax.broadcasted_iota(jnp.int32, (qs, nk), 1)
    off = kj - qi
    lane = lax.broadcasted_iota(jnp.int32, (qs, 128), 1)

    def body(i, carry):
        r0 = pl.multiple_of(i * qs, qs)
        kpos = nq * tq + i * qs - side + kj
        ok = (off >= 0) & (off <= 2 * side) & (kpos >= 0) & (kpos < n_rows)
        lse_all = jnp.zeros((qs, 128), F32)
        for h in range(DSA_HEADS):
            sl = slice(h * DSA_HEAD_DIM, (h + 1) * DSA_HEAD_DIM)
            s = _dot_nt(q_ref[0, 0, pl.ds(r0, qs), sl], kbuf[pl.ds(r0, nk), sl])
            s = jnp.where(ok, s, -1e30)
            mx = jnp.max(s, axis=-1, keepdims=True)
            p = jnp.exp(s - mx)
            den = jnp.sum(p, axis=-1, keepdims=True)
            o = _dot(p.astype(BF16), vbuf[pl.ds(r0, nk), sl])
            o_ref[0, 0, pl.ds(r0, qs), sl] = o / den
            lse_all = jnp.where(lane == h, mx + jnp.log(den), lse_all)
        lse_ref[0, 0, pl.ds(r0, qs), :] = lse_all
        return carry

    lax.fori_loop(0, tq // qs, body, 0)


def _dsa_attn(qkv, dil):
    bsz, _, n_rows, _ = qkv.shape
    tq = _pick(n_rows, 512)
    nblk = n_rows // tq
    hb = tq // DSA_SIDE
    n_halo = n_rows // DSA_SIDE

    def cur(col):
        return pl.BlockSpec((1, 1, tq, DSA_GW), lambda b, r, n: (b, r, n, col))

    def left(col):
        return pl.BlockSpec((1, 1, DSA_SIDE, DSA_GW), lambda b, r, n: (b, r, jnp.maximum(n * hb - 1, 0), col))

    def right(col):
        return pl.BlockSpec((1, 1, DSA_SIDE, DSA_GW),
                            lambda b, r, n: (b, r, jnp.minimum((n + 1) * hb, n_halo - 1), col))

    return pl.pallas_call(
        functools.partial(_dsa_attn_kernel, tq=tq, n_rows=n_rows),
        grid=(bsz, dil, nblk),
        in_specs=[cur(0), cur(1), left(1), right(1), cur(2), left(2), right(2)],
        out_specs=[
            pl.BlockSpec((1, 1, tq, DSA_GW), lambda b, r, n: (b, r, n, 0)),
            pl.BlockSpec((1, 1, tq, 128), lambda b, r, n: (b, r, n, 0)),
        ],
        out_shape=[
            jax.ShapeDtypeStruct((bsz, dil, n_rows, DSA_GW), F32),
            jax.ShapeDtypeStruct((bsz, dil, n_rows, 128), F32),
        ],
        scratch_shapes=[pltpu.VMEM((tq + 2 * DSA_SIDE, DSA_GW), BF16), pltpu.VMEM((tq + 2 * DSA_SIDE, DSA_GW), BF16)],
        compiler_params=_cparams("parallel", "parallel", "arbitrary"),
        name=f"dsa_attn_d{dil}",
    )(qkv, qkv, qkv, qkv, qkv, qkv, qkv)


def _dsa_out_kernel(o0_ref, o1_ref, o2_ref, l0_ref, l1_ref, l2_ref, w_ref, x_ref, mod_ref, o_ref, a_ref,
                    on_ref, ln_ref, *, gate_idx, dils):
    @pl.when(pl.program_id(1) == 0)
    def _():
        tm = a_ref.shape[0]
        for g, (og_ref, lg_ref) in enumerate(((o0_ref, l0_ref), (o1_ref, l1_ref), (o2_ref, l2_ref))):
            dil = dils[g]
            rows = tm // dil
            for r in range(dil):
                dst = slice(None) if dil == 1 else pl.ds(r, rows, stride=dil)
                ln_ref[g, dst, :] = lg_ref[0, r]
                for h in range(DSA_HEADS):
                    on_ref[g * DSA_HEADS + h, dst, :] = og_ref[0, r, :, h * DSA_HEAD_DIM:(h + 1) * DSA_HEAD_DIM]
        l0, l1, l2 = ln_ref[0], ln_ref[1], ln_ref[2]
        mx = jnp.maximum(jnp.maximum(l0, l1), l2)
        e0, e1, e2 = jnp.exp(l0 - mx), jnp.exp(l1 - mx), jnp.exp(l2 - mx)
        inv = 1.0 / (e0 + e1 + e2)
        w0, w1, w2 = e0 * inv, e1 * inv, e2 * inv
        for h in range(DSA_HEADS):
            merged = (w0[:, h:h + 1] * on_ref[h] + w1[:, h:h + 1] * on_ref[DSA_HEADS + h]
                      + w2[:, h:h + 1] * on_ref[2 * DSA_HEADS + h])
            a_ref[:, h * DSA_HEAD_DIM:(h + 1) * DSA_HEAD_DIM] = merged.astype(BF16)

    gate = mod_ref[0, gate_idx:gate_idx + 1, :]
    o_ref[...] = x_ref[...] + gate * _dot(a_ref[...], w_ref[...])


def _dsa_out(outs, lses, w_o, x2d, mod, seq, gate_idx):
    m, d = x2d.shape
    dw = w_o.shape[0]
    dils = tuple(o.shape[1] for o in outs)
    tm = _pick(seq, 512)
    tn = _pick(d, 512)
    blocks_per_seq = seq // tm

    def grouped(dil, width):
        return pl.BlockSpec((1, dil, tm // dil, width),
                            lambda i, j: (i // blocks_per_seq, 0, i % blocks_per_seq, 0))

    return pl.pallas_call(
        functools.partial(_dsa_out_kernel, gate_idx=gate_idx, dils=dils),
        grid=(m // tm, d // tn),
        in_specs=[grouped(dl, dw) for dl in dils] + [grouped(dl, 128) for dl in dils] + [
            pl.BlockSpec((dw, tn), lambda i, j: (0, j)),
            pl.BlockSpec((tm, tn), lambda i, j: (i, j)),
            pl.BlockSpec((1, N_MOD, tn), lambda i, j: (i // blocks_per_seq, 0, j))],
        out_specs=pl.BlockSpec((tm, tn), lambda i, j: (i, j)),
        out_shape=jax.ShapeDtypeStruct((m, d), F32),
        scratch_shapes=[pltpu.VMEM((tm, dw), BF16),
                        pltpu.VMEM((DSA_GROUPS * DSA_HEADS, tm, DSA_HEAD_DIM), F32),
                        pltpu.VMEM((DSA_GROUPS, tm, 128), F32)],
        compiler_params=_cparams("parallel", "arbitrary"),
        name="dsa_out",
    )(*outs, *lses, w_o, x2d, mod)


def _rope_tables(seq):
    pos = jnp.arange(seq, dtype=F32)
    inv = ROPE_THETA ** (-jnp.arange(0, DSA_HEAD_DIM, 2, dtype=F32) / DSA_HEAD_DIM)
    ang = pos[:, None] * inv[None, :]
    cos, sin = jnp.cos(ang), jnp.sin(ang)
    return jnp.concatenate([cos, cos], axis=-1), jnp.concatenate([-sin, sin], axis=-1)


def _dsa_mixer(x2d, mod, gain, p, bsz, seq):
    cosf, sinf = _rope_tables(seq)
    outs, lses = [], []
    for group, (_, dil) in enumerate(DSA_PATTERNS):
        qkv = _dsa_proj(x2d, mod, gain, p['w_in'][group], p['qk_gain'][group], cosf, sinf, bsz, seq, 3, dil)
        o, lse = _dsa_attn(qkv, dil)
        outs.append(o)
        lses.append(lse)
    return _dsa_out(outs, lses, p['w_o'], x2d, mod, seq, 5)


def _s5_params(w_in, lam_re, lam_im, log_dt, b_re, b_im, c_re, c_im, d_skip, w_out, bsz, seq):
    seg_len = seq // (SSM_SUB * S5_SEGMENTS)
    mw, v_mat, coef = _s5_weights(lam_re, lam_im, log_dt, b_re, b_im, c_re, c_im, d_skip, seg_len)
    return dict(w_in=w_in.astype(BF16), w_out=w_out.astype(BF16), mw=mw, v=v_mat, coef=coef)


def _gla_params(w_in, w_gate, b_gate, norm_g, w_o):
    d = w_in.shape[0]
    n_main = 2 * GLA_DK + 2 * GLA_DV
    w_code = jnp.zeros((d, 128), F32).at[:, :2 * GLA_RANK].set(w_in[:, n_main:]).astype(BF16)
    wg = jnp.zeros((2, 128, GLA_DK), F32)
    wg = wg.at[0, :GLA_RANK].set(w_gate[0]).at[1, GLA_RANK:2 * GLA_RANK].set(w_gate[1])
    wg_hi, wg_lo = _split_bf16(wg)
    return dict(w_main=w_in[:, :n_main].astype(BF16), w_code=w_code, wg_hi=wg_hi, wg_lo=wg_lo,
                bg=b_gate.reshape(2, 1, GLA_DK), norm_g=norm_g.reshape(1, GLA_DV), w_o=w_o.astype(BF16))


def _dsa_params(w_in, q_norm, k_norm, w_o):
    w4 = w_in.astype(BF16).reshape(w_in.shape[0], 3, DSA_GROUPS, DSA_GW)
    w_g = w4.transpose(2, 0, 1, 3).reshape(DSA_GROUPS, w_in.shape[0], 3 * DSA_GW)
    qk_gain = jnp.stack([q_norm * (DSA_HEAD_DIM ** -0.5), k_norm], axis=1)
    return dict(w_in=w_g, qk_gain=qk_gain.reshape(DSA_GROUPS, 2, 1, DSA_HEAD_DIM), w_o=w_o.astype(BF16))


def kernel(x, c, ada_w, ada_b, norm_g, ffn1_wi, ffn1_wo, ffn2_wi, ffn2_wo, ssm_w_in, ssm_lam_re, ssm_lam_im, ssm_log_dt, ssm_b_re, ssm_b_im, ssm_c_re, ssm_c_im, ssm_d, ssm_w_out, gla_w_in, gla_w_gate, gla_b_gate, gla_norm_g, gla_w_o, dsa_w_in, dsa_q_norm, dsa_k_norm, dsa_w_o):
    bsz, seq, d = x.shape
    depth = ada_w.shape[0]
    mods = _ada(c, ada_w, ada_b)
    x2d = x.reshape(bsz * seq, d)
    for i in range(depth):
        mod = mods[i]
        gains = norm_g[i].reshape(3, 1, d)
        act = _ffn_in(x2d, mod, gains[0], ffn1_wi[i].astype(BF16), seq, 0)
        x2d = _ffn_out(act, ffn1_wo[i].astype(BF16), x2d, mod, seq, 2)

        kind, j = i % 3, i // 3
        if kind == 0:
            p = _s5_params(ssm_w_in[j], ssm_lam_re[j], ssm_lam_im[j], ssm_log_dt[j], ssm_b_re[j], ssm_b_im[j],
                           ssm_c_re[j], ssm_c_im[j], ssm_d[j], ssm_w_out[j], bsz, seq)
            x2d = _s5_mixer(x2d, mod, gains[1], p, bsz, seq)
        elif kind == 1:
            p = _gla_params(gla_w_in[j], gla_w_gate[j], gla_b_gate[j], gla_norm_g[j], gla_w_o[j])
            x2d = _gla_mixer(x2d, mod, gains[1], p, bsz, seq)
        else:
            p = _dsa_params(dsa_w_in[j], dsa_q_norm[j], dsa_k_norm[j], dsa_w_o[j])
            x2d = _dsa_mixer(x2d, mod, gains[1], p, bsz, seq)

        act = _ffn_in(x2d, mod, gains[2], ffn2_wi[i].astype(BF16), seq, 6)
        x2d = _ffn_out(act, ffn2_wo[i].astype(BF16), x2d, mod, seq, 8)
    return x2d.reshape(bsz, seq, d)
```
